```python
import jax, jax.numpy as jnp
from jax import lax
import numpy as np

D_MODEL = 4096
BATCH = 4
SEQ = 4096
DEPTH = 1
DEC_BATCH = 1
DEC_SEQ = 16384
PAST_LEN = 128

N_DN_HEADS = 16
DN_HEAD_DIM = 128
DN_WIDTH = N_DN_HEADS * DN_HEAD_DIM
DN_CHUNK = 64
SC_WIDTH = 2048
CONV_WIDTH = 3
N_X_HEADS = 4
X_HEAD_DIM = 512
X_WIDTH = N_X_HEADS * X_HEAD_DIM
N_MEM = 256
N_BRANCH = 3
N_EXPERTS = 128
N_EXPERT_GROUPS = 8
TOPK_GROUPS = 4
TOP_K = 8
D_EXPERT = 1024
ROUTE_SCALE = 2.5
MOE_BLOCK = 256
DEEPNORM_ALPHA = (2 * DEPTH) ** 0.25
DEEPNORM_BETA = (8 * DEPTH) ** -0.25
LN_EPS = 1e-5
RMS_EPS = 1e-6

IN_SPLITS = (DN_WIDTH, DN_WIDTH, DN_WIDTH, DN_WIDTH, 2 * N_DN_HEADS, 2 * N_DN_HEADS,
             SC_WIDTH, SC_WIDTH, SC_WIDTH, X_WIDTH, N_BRANCH * D_MODEL)
IN_WIDTH = sum(IN_SPLITS)

kernel_name = 'hybrid_gdn_shortconv_memxattn_moe_encoder'


def layer_norm(x, g, b):
    xf = x.astype(jnp.float32)
    mu = jnp.mean(xf, -1, keepdims=True)
    var = jnp.mean(jnp.square(xf - mu), -1, keepdims=True)
    return ((xf - mu) * lax.rsqrt(var + LN_EPS) * g.astype(jnp.float32) + b.astype(jnp.float32)).astype(x.dtype)


def l2_normalize(x):
    return x * lax.rsqrt(jnp.sum(x * x, -1, keepdims=True) + 1e-6)


def centred_conv3(x, w):
    xp = jnp.pad(x, ((0, 0), (1, 1), (0, 0)))
    return xp[:, :-2] * w[0] + xp[:, 1:-1] * w[1] + xp[:, 2:] * w[2]


def split_columns(z):
    parts, start = [], 0
    for width in IN_SPLITS:
        parts.append(z[..., start:start + width])
        start += width
    return parts


def chunk_gated_delta(q, k, v, g, beta):
    z, s, h, dk = q.shape
    dv = v.shape[-1]
    c = DN_CHUNK
    n = s // c

    def chunks(t):
        return jnp.moveaxis(t.reshape((z, n, c, h) + t.shape[3:]), 3, 1)

    q, k, v, g, beta = (chunks(t) for t in (q, k, v, g, beta))
    gc = jnp.cumsum(g, axis=-1)
    lower = jnp.tril(jnp.ones((c, c), bool))
    strict = jnp.tril(jnp.ones((c, c), bool), -1)
    decay = jnp.exp(jnp.where(lower, gc[..., :, None] - gc[..., None, :], -jnp.inf))
    kb = k * beta[..., None]
    m = jnp.einsum('zhnid,zhnjd->zhnij', kb, k) * decay
    a = jnp.where(strict, m, 0.0) + jnp.eye(c, dtype=m.dtype)
    rhs = jnp.concatenate([v * beta[..., None], kb * jnp.exp(gc)[..., None]], -1)
    sol = lax.linalg.triangular_solve(a, rhs, left_side=True, lower=True, unit_diagonal=True)
    u, w = sol[..., :dv], sol[..., dv:]
    attn = jnp.einsum('zhnid,zhnjd->zhnij', q, k) * decay
    q_dec = q * jnp.exp(gc)[..., None]
    k_dec = k * jnp.exp(gc[..., -1:] - gc)[..., None]
    g_tot = jnp.exp(gc[..., -1])

    def step(state, xs):
        u_i, w_i, qd_i, kd_i, at_i, gt_i = xs
        v_new = u_i - jnp.einsum('zhck,zhkv->zhcv', w_i, state)
        o = jnp.einsum('zhck,zhkv->zhcv', qd_i, state) + jnp.einsum('zhij,zhjv->zhiv', at_i, v_new)
        state = state * gt_i[..., None, None] + jnp.einsum('zhck,zhcv->zhkv', kd_i, v_new)
        return state, o

    xs = tuple(jnp.moveaxis(t, 2, 0) for t in (u, w, q_dec, k_dec, attn, g_tot))
    _, o = lax.scan(step, jnp.zeros((z, h, dk, dv), jnp.float32), xs)
    return jnp.transpose(o, (1, 0, 3, 2, 4)).reshape(z, s, h, dv)


def memory_cross_attention(xq, mem, w_kv):
    bsz, s, _ = xq.shape
    n_mem = mem.shape[1]
    q = xq.reshape(bsz, s, N_X_HEADS, X_HEAD_DIM)
    kv = jnp.einsum('bmd,dn->bmn', mem, w_kv).reshape(bsz, n_mem, 2, N_X_HEADS, X_HEAD_DIM)
    k, v = kv[:, :, 0], kv[:, :, 1]
    scores = jnp.einsum('bshd,bmhd->bhsm', q, k).astype(jnp.float32) * X_HEAD_DIM ** -0.5
    p = jax.nn.softmax(scores, axis=-1).astype(v.dtype)
    return jnp.einsum('bhsm,bmhd->bshd', p, v).reshape(bsz, s, X_WIDTH)


def moe_ffn(x, layer, w_router, router_bias, w_gate_e, w_up_e, w_down_e, w_gate_s, w_up_s, w_down_s):
    bsz, s, d = x.shape
    t = bsz * s
    xf = x.reshape(t, d)
    scores = jax.nn.sigmoid(jnp.einsum('td,de->te', xf, w_router[layer]).astype(jnp.float32))
    sel = scores + router_bias[layer].astype(jnp.float32)
    grp_scores = jnp.sum(lax.top_k(sel.reshape(t, N_EXPERT_GROUPS, N_EXPERTS // N_EXPERT_GROUPS), 2)[0], -1)
    _, grp_idx = lax.top_k(grp_scores, TOPK_GROUPS)
    grp_keep = jnp.any(grp_idx[:, :, None] == jnp.arange(N_EXPERT_GROUPS), axis=1)
    expert_keep = jnp.repeat(grp_keep, N_EXPERTS // N_EXPERT_GROUPS, axis=1)
    _, eidx = lax.top_k(jnp.where(expert_keep, sel, -jnp.inf), TOP_K)
    gate = jnp.take_along_axis(scores, eidx, axis=1)
    gate = gate / jnp.sum(gate, -1, keepdims=True) * ROUTE_SCALE
    flat_e = eidx.reshape(-1)
    order = jnp.argsort(flat_e)
    sorted_e = flat_e[order]
    counts = jnp.bincount(flat_e, length=N_EXPERTS)
    padded = (counts + MOE_BLOCK - 1) // MOE_BLOCK * MOE_BLOCK
    starts = jnp.cumsum(counts) - counts
    padded_ends = jnp.cumsum(padded)
    padded_starts = padded_ends - padded
    dest = padded_starts[sorted_e] + jnp.arange(t * TOP_K, dtype=jnp.int32) - starts[sorted_e]
    n_blocks = (t * TOP_K + N_EXPERTS * (MOE_BLOCK - 1) + MOE_BLOCK - 1) // MOE_BLOCK
    rows = n_blocks * MOE_BLOCK
    row_tok = jnp.full((rows,), t, jnp.int32).at[dest].set((jnp.arange(t * TOP_K, dtype=jnp.int32) // TOP_K)[order])
    row_gate = jnp.zeros((rows,), jnp.float32).at[dest].set(gate.reshape(-1)[order])
    block_expert = jnp.minimum(jnp.searchsorted(padded_ends, jnp.arange(n_blocks, dtype=jnp.int32) * MOE_BLOCK, side='right'), N_EXPERTS - 1)
    x_pad = jnp.concatenate([xf, jnp.zeros((1, d), xf.dtype)], 0)

    def expert_block(acc, blk):
        tok, wt, e = blk
        xb = x_pad[tok]
        hb = jax.nn.silu(xb @ w_gate_e[layer, e]) * (xb @ w_up_e[layer, e])
        yb = (hb @ w_down_e[layer, e]) * wt[:, None].astype(xf.dtype)
        return acc.at[tok].add(yb), None

    routed, _ = lax.scan(expert_block, jnp.zeros((t + 1, d), xf.dtype),
                         (row_tok.reshape(n_blocks, MOE_BLOCK), row_gate.reshape(n_blocks, MOE_BLOCK), block_expert))
    shared = (jax.nn.silu(xf @ w_gate_s[layer]) * (xf @ w_up_s[layer])) @ w_down_s[layer]
    return (routed[:t] + shared).reshape(bsz, s, d)


def encoder_layer(x, mem, layer, w_in, w_dn_conv, dn_a_log, dn_dt_bias, dn_norm_w, w_sc_conv, w_mem_kv,
                  w_branch, w_o, ln1_g, ln1_b, w_router, router_bias, w_gate_e, w_up_e, w_down_e,
                  w_gate_s, w_up_s, w_down_s, ln2_g, ln2_b):
    f32 = jnp.float32
    bsz, s, _ = x.shape
    z = jnp.einsum('bsd,dn->bsn', x, w_in[layer])
    (q_in, k_in, v_in, og_in, a_in, b_in, bg_in, cg_in, h_in, xq_in, mg_in) = split_columns(z)
    qkv = jax.nn.silu(centred_conv3(jnp.concatenate([q_in, k_in, v_in], -1), w_dn_conv[layer]))
    hd = (bsz, s, N_DN_HEADS, DN_HEAD_DIM)
    q = l2_normalize(qkv[..., :DN_WIDTH].astype(f32).reshape(hd)) * DN_HEAD_DIM ** -0.5
    k = l2_normalize(qkv[..., DN_WIDTH:2 * DN_WIDTH].astype(f32).reshape(hd))
    v = qkv[..., 2 * DN_WIDTH:].astype(f32).reshape(hd)
    a = a_in.astype(f32).reshape(bsz, s, 2, N_DN_HEADS)
    g = -jnp.exp(dn_a_log[layer].astype(f32)) * jax.nn.softplus(a + dn_dt_bias[layer].astype(f32))
    beta = jax.nn.sigmoid(b_in.astype(f32).reshape(bsz, s, 2, N_DN_HEADS))
    both = lambda fwd, bwd: jnp.concatenate([fwd, jnp.flip(bwd, 1)], 0)
    o = chunk_gated_delta(both(q, q), both(k, k), both(v, v), both(g[:, :, 0], g[:, :, 1]),
                          both(beta[:, :, 0], beta[:, :, 1]))
    o = o[:bsz] + jnp.flip(o[bsz:], 1)
    o = (o * lax.rsqrt(jnp.mean(o * o, -1, keepdims=True) + RMS_EPS) * dn_norm_w[layer].astype(f32)
         * jax.nn.silu(og_in.astype(f32).reshape(hd)))
    y_dn = o.reshape(bsz, s, DN_WIDTH).astype(x.dtype)
    y_sc = bg_in * centred_conv3(cg_in * h_in, w_sc_conv[layer])
    y_mem = memory_cross_attention(xq_in, mem, w_mem_kv[layer])
    gate_logits = mg_in.reshape(bsz, s, N_BRANCH, D_MODEL)
    mixed = sum(jax.nn.sigmoid(gate_logits[:, :, i]) * jnp.einsum('bsw,wd->bsd', y_i, w_branch[layer, i])
                for i, y_i in enumerate((y_dn, y_sc, y_mem)))
    h = layer_norm(DEEPNORM_ALPHA * x + jnp.einsum('bsd,de->bse', mixed, w_o[layer]), ln1_g[layer], ln1_b[layer])
    f = moe_ffn(h, layer, w_router, router_bias, w_gate_e, w_up_e, w_down_e, w_gate_s, w_up_s, w_down_s)
    return layer_norm(DEEPNORM_ALPHA * h + f, ln2_g[layer], ln2_b[layer])


def setup_inputs(seed: int = 0) -> dict:
    key = jax.random.key(seed)
    ks = iter(jax.random.split(key, 40))
    f32 = jnp.float32

    def nrm(shape, scale):
        return jax.random.normal(next(ks), shape, f32) * scale

    dt = jnp.exp(jax.random.uniform(next(ks), (DEPTH, 2, N_DN_HEADS), f32, np.log(1e-3), np.log(1e-1)))
    return {
        'x_prompt': nrm((BATCH, SEQ, D_MODEL), 1.0),
        'x_sample': nrm((DEC_BATCH, DEC_SEQ, D_MODEL), 1.0),
        'mem_prompt': nrm((BATCH, N_MEM, D_MODEL), 1.0),
        'mem_sample': nrm((DEC_BATCH, N_MEM, D_MODEL), 1.0),
        'ln_in_g': 1.0 + nrm((D_MODEL,), 0.02),
        'ln_in_b': nrm((D_MODEL,), 0.02),
        'w_in': nrm((DEPTH, D_MODEL, IN_WIDTH), D_MODEL ** -0.5),
        'w_dn_conv': nrm((DEPTH, CONV_WIDTH, 3 * DN_WIDTH), CONV_WIDTH ** -0.5),
        'dn_a_log': jnp.log(jax.random.uniform(next(ks), (DEPTH, 2, N_DN_HEADS), f32, 1.0, 16.0)),
        'dn_dt_bias': dt + jnp.log(-jnp.expm1(-dt)),
        'dn_norm_w': 1.0 + nrm((DEPTH, DN_HEAD_DIM), 0.02),
        'w_sc_conv': nrm((DEPTH, CONV_WIDTH, SC_WIDTH), CONV_WIDTH ** -0.5),
        'w_mem_kv': nrm((DEPTH, D_MODEL, 2 * X_WIDTH), D_MODEL ** -0.5),
        'w_branch': nrm((DEPTH, N_BRANCH, DN_WIDTH, D_MODEL), DN_WIDTH ** -0.5),
        'w_o': nrm((DEPTH, D_MODEL, D_MODEL), D_MODEL ** -0.5 * DEEPNORM_BETA),
        'ln1_g': 1.0 + nrm((DEPTH, D_MODEL), 0.02),
        'ln1_b': nrm((DEPTH, D_MODEL), 0.02),
        'w_router': nrm((DEPTH, D_MODEL, N_EXPERTS), D_MODEL ** -0.5),
        'router_bias': nrm((DEPTH, N_EXPERTS), 0.01),
        'w_gate_e': nrm((DEPTH, N_EXPERTS, D_MODEL, D_EXPERT), D_MODEL ** -0.5),
        'w_up_e': nrm((DEPTH, N_EXPERTS, D_MODEL, D_EXPERT), D_MODEL ** -0.5),
        'w_down_e': nrm((DEPTH, N_EXPERTS, D_EXPERT, D_MODEL), D_EXPERT ** -0.5 * DEEPNORM_BETA),
        'w_gate_s': nrm((DEPTH, D_MODEL, D_EXPERT), D_MODEL ** -0.5),
        'w_up_s': nrm((DEPTH, D_MODEL, D_EXPERT), D_MODEL ** -0.5),
        'w_down_s': nrm((DEPTH, D_EXPERT, D_MODEL), D_EXPERT ** -0.5 * DEEPNORM_BETA),
        'ln2_g': 1.0 + nrm((DEPTH, D_MODEL), 0.02),
        'ln2_b': nrm((DEPTH, D_MODEL), 0.02),
    }


def reference(x_prompt, x_sample, mem_prompt, mem_sample, ln_in_g, ln_in_b, w_in, w_dn_conv, dn_a_log,
              dn_dt_bias, dn_norm_w, w_sc_conv, w_mem_kv, w_branch, w_o, ln1_g, ln1_b, w_router, router_bias,
              w_gate_e, w_up_e, w_down_e, w_gate_s, w_up_s, w_down_s, ln2_g, ln2_b):
    def trunk(x, mem):
        h = layer_norm(x, ln_in_g, ln_in_b)
        for layer in range(DEPTH):
            h = encoder_layer(h, mem, layer, w_in, w_dn_conv, dn_a_log, dn_dt_bias, dn_norm_w, w_sc_conv,
                              w_mem_kv, w_branch, w_o, ln1_g, ln1_b, w_router, router_bias, w_gate_e, w_up_e,
                              w_down_e, w_gate_s, w_up_s, w_down_s, ln2_g, ln2_b)
        return h

    y_prompt = trunk(x_prompt, mem_prompt)
    y_sample = trunk(x_sample, mem_sample)
    return (y_prompt, y_sample)
```

```python
import functools

import jax
import jax.numpy as jnp
from jax import lax
from jax.experimental import pallas as pl
from jax.experimental.pallas import tpu as pltpu

F32 = jnp.float32
BF16 = jnp.bfloat16
I32 = jnp.int32
U32 = jnp.uint32

DN_CHUNK = 64
N_X_HEADS = 4
N_EXPERT_GROUPS = 8
TOPK_GROUPS = 4
TOP_K = 8
ROUTE_SCALE = 2.5
DEPTH = 1
DEEPNORM_ALPHA = (2 * DEPTH) ** 0.25
LN_EPS = 1e-5
RMS_EPS = 1e-6
L2_EPS = 1e-6

V7X_VMEM_BYTES = 64 * 1024 * 1024
VMEM_LIMIT = 56 * 1024 * 1024
LANES = 128
HALO = 16


def _params(sem):
    return pltpu.CompilerParams(dimension_semantics=sem, vmem_limit_bytes=VMEM_LIMIT)


def _sigmoid(x):
    return 1.0 / (1.0 + jnp.exp(-x))


def _silu(x):
    return x * _sigmoid(x)


def _softplus(x):
    return jnp.maximum(x, 0.0) + jnp.log1p(jnp.exp(-jnp.abs(x)))


def _tile(n, pref):
    while n % pref:
        pref //= 2
    return pref


def _layer_norm(x, g, b):
    mu = jnp.mean(x, -1, keepdims=True)
    xc = x - mu
    var = jnp.mean(xc * xc, -1, keepdims=True)
    return xc * lax.rsqrt(var + LN_EPS) * g + b


def _pack_bf16_pair(lo, hi):
    lo_b = lax.bitcast_convert_type(lo.astype(BF16).astype(F32), U32)
    hi_b = lax.bitcast_convert_type(hi.astype(BF16).astype(F32), U32)
    return (hi_b & jnp.uint32(0xFFFF0000)) | (lo_b >> 16)


def _unpack_bf16_pair(w):
    lo = lax.bitcast_convert_type(w << 16, F32)
    hi = lax.bitcast_convert_type(w & jnp.uint32(0xFFFF0000), F32)
    return lo, hi


def _ln_in_kernel(half_blocks, xp_ref, xs_ref, g_ref, b_ref, of_ref, ob_ref):
    i = pl.program_id(0)

    def emit(x_ref):
        y = _layer_norm(x_ref[...], g_ref[...], b_ref[...])
        of_ref[...] = y
        ob_ref[...] = y.astype(BF16)

    @pl.when(i < half_blocks)
    def _():
        emit(xp_ref)

    @pl.when(i >= half_blocks)
    def _():
        emit(xs_ref)


def ln_in(xp, xs, g, b, tm=256):
    tp, d = xp.shape
    ts = xs.shape[0]
    hb = tp // tm
    nb = hb + ts // tm
    return pl.pallas_call(
        functools.partial(_ln_in_kernel, hb),
        grid=(nb,),
        in_specs=[
            pl.BlockSpec((tm, d), lambda i: (jnp.minimum(i, hb - 1), 0)),
            pl.BlockSpec((tm, d), lambda i: (jnp.maximum(i - hb, 0), 0)),
            pl.BlockSpec((1, d), lambda i: (0, 0)),
            pl.BlockSpec((1, d), lambda i: (0, 0)),
        ],
        out_specs=[pl.BlockSpec((tm, d), lambda i: (i, 0)), pl.BlockSpec((tm, d), lambda i: (i, 0))],
        out_shape=[jax.ShapeDtypeStruct((tp + ts, d), F32), jax.ShapeDtypeStruct((tp + ts, d), BF16)],
        compiler_params=_params(("arbitrary",)),
        name="ln_in",
    )(xp, xs, g.reshape(1, d), b.reshape(1, d))


def _mm_kernel(a_ref, b_ref, o_ref):
    o_ref[...] = jnp.dot(a_ref[...], b_ref[...], preferred_element_type=F32).astype(o_ref.dtype)


def matmul(a, b, tm, tn, out_dtype, name):
    m, k = a.shape
    n = b.shape[1]
    tm, tn = _tile(m, tm), _tile(n, tn)
    return pl.pallas_call(
        _mm_kernel,
        grid=(m // tm, n // tn),
        in_specs=[pl.BlockSpec((tm, k), lambda i, j: (i, 0)), pl.BlockSpec((k, tn), lambda i, j: (0, j))],
        out_specs=pl.BlockSpec((tm, tn), lambda i, j: (i, j)),
        out_shape=jax.ShapeDtypeStruct((m, n), out_dtype),
        compiler_params=_params(("arbitrary", "arbitrary")),
        name=name,
    )(a, b)


def _dn_gates_kernel(chunk, nh, h_ref, w_ref, wt_ref, alog_ref, dtb_ref, alogt_ref, dtbt_ref, col_ref, row_ref):
    tm = h_ref.shape[0]
    h = h_ref[...]
    zc = jnp.dot(h, w_ref[...], preferred_element_type=F32)
    zr = lax.dot_general(wt_ref[...], h, (((1,), (1,)), ((), ())), preferred_element_type=F32)
    r = lax.broadcasted_iota(I32, (tm, tm), 0)
    c = lax.broadcasted_iota(I32, (tm, tm), 1)
    same = (r // chunk) == (c // chunk)
    le = jnp.where(same & (c <= r), 1.0, 0.0).astype(F32)
    ge = jnp.where(same & (c >= r), 1.0, 0.0).astype(F32)
    hp = lax.Precision.HIGHEST
    for d in range(2):
        a_c = zc[:, d * nh:(d + 1) * nh]
        b_c = zc[:, (2 + d) * nh:(3 + d) * nh]
        g_c = -jnp.exp(alog_ref[:, d * nh:(d + 1) * nh]) * _softplus(a_c + dtb_ref[:, d * nh:(d + 1) * nh])
        a_r = zr[d * nh:(d + 1) * nh, :]
        b_r = zr[(2 + d) * nh:(3 + d) * nh, :]
        g_r = -jnp.exp(alogt_ref[d * nh:(d + 1) * nh, :]) * _softplus(a_r + dtbt_ref[d * nh:(d + 1) * nh, :])
        incl, rest = (le, ge) if d == 0 else (ge, le)
        gc_c = jnp.dot(incl, g_c, preferred_element_type=F32, precision=hp)
        gr_c = jnp.dot(rest, g_c, preferred_element_type=F32, precision=hp) - g_c
        gc_r = jnp.dot(g_r, rest, preferred_element_type=F32, precision=hp)
        gr_r = jnp.dot(g_r, incl, preferred_element_type=F32, precision=hp) - g_r
        col_ref[d] = jnp.concatenate([gc_c, _sigmoid(b_c), gr_c], axis=1)
        row_ref[d] = jnp.concatenate([gc_r, _sigmoid(b_r), gr_r], axis=0)


def dn_gates(h_bf, w_ab, a_log, dt_bias, nh, chunk, tm=256):
    t, d = h_bf.shape
    w_t = w_ab.T
    alog = a_log.reshape(1, 2 * nh).astype(F32)
    dtb = dt_bias.reshape(1, 2 * nh).astype(F32)
    full = lambda shape: pl.BlockSpec(shape, lambda i: (0,) * len(shape))
    return pl.pallas_call(
        functools.partial(_dn_gates_kernel, chunk, nh),
        grid=(t // tm,),
        in_specs=[pl.BlockSpec((tm, d), lambda i: (i, 0)), full((d, 4 * nh)), full((4 * nh, d)),
                  full((1, 2 * nh)), full((1, 2 * nh)), full((2 * nh, 1)), full((2 * nh, 1))],
        out_specs=[pl.BlockSpec((2, tm, 3 * nh), lambda i: (0, i, 0)),
                   pl.BlockSpec((2, 3 * nh, tm), lambda i: (0, 0, i))],
        out_shape=[jax.ShapeDtypeStruct((2, t, 3 * nh), F32), jax.ShapeDtypeStruct((2, 3 * nh, t), F32)],
        compiler_params=_params(("arbitrary",)),
        name="dn_gates",
    )(h_bf, w_ab, w_t, alog, dtb, alog.T, dtb.T)


def _seq_edges(tok0, tm, seq_a, n_a, total):
    end = tok0 + tm
    starts = ((tok0 % seq_a == 0) & (tok0 <= n_a))
    ends = ((end % seq_a == 0) & (end <= n_a)) | (end == total)
    return starts, ends


def _shifted(x, prev_row, next_row):
    tm = x.shape[0]
    rows = lax.broadcasted_iota(I32, x.shape, 0)
    xm = jnp.where(rows == 0, prev_row, pltpu.roll(x, 1, 0))
    xp = jnp.where(rows == tm - 1, next_row, pltpu.roll(x, tm - 1, 0))
    return xm, xp


def _halo_specs(tm, width, col_block, n_rows):
    per = tm // HALO
    last = n_rows // HALO - 1
    return [
        pl.BlockSpec((tm, width), lambda i, *_: (i, col_block(i, *_))),
        pl.BlockSpec((HALO, width), lambda i, *_: (jnp.maximum(i * per - 1, 0), col_block(i, *_))),
        pl.BlockSpec((HALO, width), lambda i, *_: (jnp.minimum((i + 1) * per, last), col_block(i, *_))),
    ]


def _dn_prep_kernel(seq_a, n_a, total, hd, x_ref, xprev_ref, xnext_ref, w_ref, o_ref):
    i = pl.program_id(0)
    j = pl.program_id(1)
    tm, width = x_ref.shape
    starts, ends = _seq_edges(i * tm, tm, seq_a, n_a, total)
    x = x_ref[...].astype(F32)
    prev_row = jnp.where(starts, 0.0, xprev_ref[HALO - 1:HALO, :].astype(F32))
    next_row = jnp.where(ends, 0.0, xnext_ref[0:1, :].astype(F32))
    xm, xp = _shifted(x, prev_row, next_row)
    y = _silu(xm * w_ref[0:1, :] + x * w_ref[1:2, :] + xp * w_ref[2:3, :])

    @pl.when(j == 2)
    def _():
        o_ref[...] = y.astype(o_ref.dtype)

    @pl.when(j < 2)
    def _():
        scale = jnp.where(j == 0, hd ** -0.5, 1.0).astype(F32)
        for h in range(width // hd):
            yh = y[:, h * hd:(h + 1) * hd]
            inv = lax.rsqrt(jnp.sum(yh * yh, -1, keepdims=True) + L2_EPS) * scale
            o_ref[:, h * hd:(h + 1) * hd] = (yh * inv).astype(o_ref.dtype)


def dn_prep(z, w_conv, width, hd, seq_a, n_a, tm=256):
    t = z.shape[0]
    return pl.pallas_call(
        functools.partial(_dn_prep_kernel, seq_a, n_a, t, hd),
        grid=(t // tm, 3),
        in_specs=_halo_specs(tm, width, lambda i, j: j, t) + [pl.BlockSpec((3, width), lambda i, j: (0, j))],
        out_specs=pl.BlockSpec((tm, width), lambda i, j: (i, j)),
        out_shape=jax.ShapeDtypeStruct((t, 3 * width), BF16),
        compiler_params=_params(("arbitrary", "arbitrary")),
        name="dn_prep",
    )(z, z, z, w_conv)


def _delta_kernel(seq_chunks_a, chunks_a, n_chunks, nh, hd,
                  q_ref, k_ref, v_ref, gcol_ref, grow_ref, o_ref, s_ref):
    d = pl.program_id(0)
    i = pl.program_id(1)
    c = jnp.where(d == 0, i, n_chunks - 1 - i)
    cs = q_ref.shape[0]
    first_fwd = ((c % seq_chunks_a == 0) & (c <= chunks_a))
    first_bwd = (((c + 1) % seq_chunks_a == 0) & (c + 1 <= chunks_a)) | (c + 1 == n_chunks)
    first = ((d == 0) & first_fwd) | ((d != 0) & first_bwd)

    @pl.when(first)
    def _():
        s_ref[...] = jnp.zeros_like(s_ref)

    r = lax.broadcasted_iota(I32, (cs, cs), 0)
    cc = lax.broadcasted_iota(I32, (cs, cs), 1)
    ahead = (r - cc) * jnp.where(d == 0, 1, -1)
    tri = ahead >= 0
    strict = ahead > 0
    eye = (r == cc).astype(F32)
    dot = functools.partial(jnp.dot, preferred_element_type=F32)

    for h in range(nh):
        hs = slice(h * hd, (h + 1) * hd)
        qh = q_ref[:, hs]
        kh = k_ref[:, hs]
        vh = v_ref[:, hs].astype(F32)
        khf = kh.astype(F32)
        gc = gcol_ref[:, h:h + 1]
        beta = gcol_ref[:, nh + h:nh + h + 1]
        gr = gcol_ref[:, 2 * nh + h:2 * nh + h + 1]
        gc_row = grow_ref[h:h + 1, :]
        decay = jnp.exp(jnp.where(tri, gc - gc_row, -jnp.inf))
        kq = lax.dot_general(jnp.concatenate([kh, qh], axis=0), kh, (((1,), (1,)), ((), ())),
                             preferred_element_type=F32)
        kk = kq[:cs]
        qk = kq[cs:]
        low = jnp.where(strict, beta * kk * decay, 0.0)
        inv = eye - low
        x = low
        n_sq = max(1, (cs - 1).bit_length() - 1)
        for _ in range(n_sq):
            xb = x.astype(BF16)
            x = dot(xb, xb)
            inv = inv + dot(inv.astype(BF16), x.astype(BF16))
        eg = jnp.exp(gc)
        rhs = jnp.concatenate([vh * beta, khf * (beta * eg)], axis=1).astype(BF16)
        uw = dot(inv.astype(BF16), rhs)
        u = uw[:, :hd]
        w = uw[:, hd:]
        attn = (qk * decay).astype(BF16)
        s = s_ref[h]
        sb = s.astype(BF16)
        wq = jnp.concatenate([w, qh.astype(F32) * eg], axis=0).astype(BF16)
        wqs = dot(wq, sb)
        v_new = u - wqs[:cs]
        vb = v_new.astype(BF16)
        o_ref[:, hs] = wqs[cs:] + dot(attn, vb)
        kd = (khf * jnp.exp(gr)).astype(BF16)
        g_tot = jnp.exp(gc[0:1, :] + gr[0:1, :])
        s_ref[h] = s * g_tot + lax.dot_general(kd, vb, (((0,), (0,)), ((), ())), preferred_element_type=F32)


def delta_rule(qkv, gcol, grow, nh, hd, seq_a, n_a):
    t = qkv.shape[0]
    width = nh * hd
    cs = DN_CHUNK
    n_chunks = t // cs
    ng = gcol.shape[-1]
    grow = grow.reshape(2, ng, n_chunks, cs).transpose(0, 2, 1, 3)
    chunk_of = lambda d, i: jnp.where(d == 0, i, n_chunks - 1 - i)
    return pl.pallas_call(
        functools.partial(_delta_kernel, seq_a // cs, n_a // cs, n_chunks, nh, hd),
        grid=(2, n_chunks),
        in_specs=[
            pl.BlockSpec((cs, width), lambda d, i: (chunk_of(d, i), 0)),
            pl.BlockSpec((cs, width), lambda d, i: (chunk_of(d, i), 1)),
            pl.BlockSpec((cs, width), lambda d, i: (chunk_of(d, i), 2)),
            pl.BlockSpec((None, cs, ng), lambda d, i: (d, chunk_of(d, i), 0)),
            pl.BlockSpec((None, None, ng, cs), lambda d, i: (d, chunk_of(d, i), 0, 0)),
        ],
        out_specs=pl.BlockSpec((None, cs, width), lambda d, i: (d, chunk_of(d, i), 0)),
        out_shape=jax.ShapeDtypeStruct((2, t, width), F32),
        scratch_shapes=[pltpu.VMEM((nh, hd, hd), F32)],
        compiler_params=_params(("arbitrary", "arbitrary")),
        name="delta_rule",
    )(qkv, qkv, qkv, gcol, grow)


def _dn_post_kernel(hd, col_block, o_ref, og_ref, nw_ref, y_ref):
    del col_block
    o = o_ref[0] + o_ref[1]
    og = og_ref[...].astype(F32)
    nw = nw_ref[...]
    for h in range(o.shape[1] // hd):
        hs = slice(h * hd, (h + 1) * hd)
        oh = o[:, hs]
        inv = lax.rsqrt(jnp.mean(oh * oh, -1, keepdims=True) + RMS_EPS)
        y_ref[:, hs] = (oh * inv * nw * _silu(og[:, hs])).astype(y_ref.dtype)


def dn_post(o2, z, og_block, norm_w, hd, tm=256):
    _, t, width = o2.shape
    return pl.pallas_call(
        functools.partial(_dn_post_kernel, hd, og_block),
        grid=(t // tm,),
        in_specs=[pl.BlockSpec((2, tm, width), lambda i: (0, i, 0)),
                  pl.BlockSpec((tm, width), lambda i: (i, og_block)),
                  pl.BlockSpec((1, hd), lambda i: (0, 0))],
        out_specs=pl.BlockSpec((tm, width), lambda i: (i, 0)),
        out_shape=jax.ShapeDtypeStruct((t, width), BF16),
        compiler_params=_params(("arbitrary",)),
        name="dn_post",
    )(o2, z, norm_w.reshape(1, hd).astype(F32))


def _sc_kernel(seq_a, n_a, total, b_ref, c_ref, cprev_ref, cnext_ref, h_ref, hprev_ref, hnext_ref, w_ref, y_ref):
    i = pl.program_id(0)
    tm = b_ref.shape[0]
    starts, ends = _seq_edges(i * tm, tm, seq_a, n_a, total)
    x = c_ref[...].astype(F32) * h_ref[...].astype(F32)
    prev_row = jnp.where(starts, 0.0, cprev_ref[HALO - 1:HALO, :].astype(F32) * hprev_ref[HALO - 1:HALO, :].astype(F32))
    next_row = jnp.where(ends, 0.0, cnext_ref[0:1, :].astype(F32) * hnext_ref[0:1, :].astype(F32))
    xm, xp = _shifted(x, prev_row, next_row)
    conv = xm * w_ref[0:1, :] + x * w_ref[1:2, :] + xp * w_ref[2:3, :]
    y_ref[...] = (b_ref[...].astype(F32) * conv).astype(y_ref.dtype)


def short_conv(z, w_conv, first_block, width, seq_a, n_a, tm=256):
    t = z.shape[0]
    return pl.pallas_call(
        functools.partial(_sc_kernel, seq_a, n_a, t),
        grid=(t // tm,),
        in_specs=([pl.BlockSpec((tm, width), lambda i: (i, first_block))]
                  + _halo_specs(tm, width, lambda i: first_block + 1, t)
                  + _halo_specs(tm, width, lambda i: first_block + 2, t)
                  + [pl.BlockSpec((3, width), lambda i: (0, 0))]),
        out_specs=pl.BlockSpec((tm, width), lambda i: (i, 0)),
        out_shape=jax.ShapeDtypeStruct((t, width), BF16),
        compiler_params=_params(("arbitrary",)),
        name="short_conv",
    )(z, z, z, z, z, z, z, w_conv)


def _xattn_kernel(scale, q_ref, k_ref, v_ref, o_ref):
    s = lax.dot_general(q_ref[...], k_ref[...], (((1,), (1,)), ((), ())), preferred_element_type=F32) * scale
    m = jnp.max(s, -1, keepdims=True)
    e = jnp.exp(s - m)
    p = e / jnp.sum(e, -1, keepdims=True)
    o_ref[...] = jnp.dot(p.astype(BF16), v_ref[...], preferred_element_type=F32).astype(o_ref.dtype)


def mem_attention(z, kv, q_block0, n_heads, hd, n_mem, seq_a, n_a, tm=512):
    t = z.shape[0]
    n_seq_a = n_a // seq_a

    def seq_of(i):
        tok = i * tm
        return jnp.where(tok < n_a, tok // seq_a, n_seq_a)

    return pl.pallas_call(
        functools.partial(_xattn_kernel, hd ** -0.5),
        grid=(t // tm, n_heads),
        in_specs=[pl.BlockSpec((tm, hd), lambda i, h: (i, q_block0 + h)),
                  pl.BlockSpec((n_mem, hd), lambda i, h: (seq_of(i), h)),
                  pl.BlockSpec((n_mem, hd), lambda i, h: (seq_of(i), n_heads + h))],
        out_specs=pl.BlockSpec((tm, hd), lambda i, h: (i, h)),
        out_shape=jax.ShapeDtypeStruct((t, n_heads * hd), BF16),
        compiler_params=_params(("arbitrary", "arbitrary")),
        name="mem_attention",
    )(z, kv, kv)


def _merge_kernel(nb, *refs):
    y_refs = refs[:nb]
    g_refs = refs[nb:2 * nb]
    w_ref = refs[2 * nb]
    o_ref = refs[2 * nb + 1]
    acc = None
    for b in range(nb):
        term = _sigmoid(g_refs[b][...].astype(F32)) * jnp.dot(y_refs[b][...], w_ref[b], preferred_element_type=F32)
        acc = term if acc is None else acc + term
    o_ref[...] = acc.astype(o_ref.dtype)


def gated_merge(ys, z, gate_col0, w_branch, tm=512, tn=512):
    nb = len(ys)
    t, width = ys[0].shape
    d = w_branch.shape[-1]
    g0 = gate_col0 // tn
    per = d // tn
    gate_spec = lambda b: pl.BlockSpec((tm, tn), lambda i, j: (i, g0 + b * per + j))
    return pl.pallas_call(
        functools.partial(_merge_kernel, nb),
        grid=(t // tm, d // tn),
        in_specs=([pl.BlockSpec((tm, width), lambda i, j: (i, 0))] * nb
                  + [gate_spec(b) for b in range(nb)]
                  + [pl.BlockSpec((nb, width, tn), lambda i, j: (0, 0, j))]),
        out_specs=pl.BlockSpec((tm, tn), lambda i, j: (i, j)),
        out_shape=jax.ShapeDtypeStruct((t, d), BF16),
        compiler_params=_params(("arbitrary", "arbitrary")),
        name="gated_merge",
    )(*ys, *([z] * nb), w_branch)


def _res_ln1_kernel(x_ref, y_ref, g_ref, b_ref, of_ref, ob_ref, op_ref):
    h = _layer_norm(DEEPNORM_ALPHA * x_ref[...] + y_ref[...], g_ref[...], b_ref[...])
    of_ref[...] = h
    ob_ref[...] = h.astype(BF16)
    half = h.shape[1] // 2
    op_ref[...] = _pack_bf16_pair(h[:, :half], h[:, half:])


def res_ln1(x, y, g, b, tm=256):
    t, d = x.shape
    row = lambda w: pl.BlockSpec((tm, w), lambda i: (i, 0))
    vec = pl.BlockSpec((1, d), lambda i: (0, 0))
    return pl.pallas_call(
        _res_ln1_kernel,
        grid=(t // tm,),
        in_specs=[row(d), row(d), vec, vec],
        out_specs=[row(d), row(d), row(d // 2)],
        out_shape=[jax.ShapeDtypeStruct((t, d), F32), jax.ShapeDtypeStruct((t, d), BF16),
                   jax.ShapeDtypeStruct((t, d // 2), U32)],
        compiler_params=_params(("arbitrary",)),
        name="res_ln1",
    )(x, y, g.reshape(1, d), b.reshape(1, d))


def _first_index_of_max(x, idx, big):
    m = jnp.max(x, axis=0, keepdims=True)
    return m, jnp.min(jnp.where(x == m, idx, big), axis=0, keepdims=True)


def _router_kernel(ne, ng, kg, topk, h_ref, wt_ref, bias_ref, eidx_ref, gate_ref, rank_ref, cnt_ref, run_ref):
    i = pl.program_id(0)
    tm = h_ref.shape[0]
    per = ne // ng

    @pl.when(i == 0)
    def _():
        run_ref[...] = jnp.zeros_like(run_ref)

    logits = lax.dot_general(wt_ref[...], h_ref[...], (((1,), (1,)), ((), ())),
                             preferred_element_type=F32, precision=lax.Precision.HIGHEST)
    scores = _sigmoid(logits)
    sel = scores + bias_ref[...]
    eid = lax.broadcasted_iota(I32, (ne, tm), 0)
    neg = -jnp.inf
    pid = lax.broadcasted_iota(I32, (per, tm), 0)
    gscores = []
    for g in range(ng):
        sg = sel[g * per:(g + 1) * per, :]
        m1, a1 = _first_index_of_max(sg, pid, per)
        m2 = jnp.max(jnp.where(pid == a1, neg, sg), axis=0, keepdims=True)
        gscores.append(m1 + m2)
    gscore = jnp.concatenate(gscores, axis=0)
    gid = lax.broadcasted_iota(I32, (ng, tm), 0)
    egroup = eid // per
    keep_e = jnp.zeros((ne, tm), jnp.bool_)
    for _ in range(kg):
        _, a = _first_index_of_max(gscore, gid, ng)
        keep_e = keep_e | (egroup == a)
        gscore = jnp.where(gid == a, neg, gscore)
    cand = jnp.where(keep_e, sel, neg)
    onehot = jnp.zeros((ne, tm), F32)
    idxs, gates = [], []
    for _ in range(topk):
        _, a = _first_index_of_max(cand, eid, ne)
        hit = eid == a
        idxs.append(a)
        gates.append(jnp.sum(jnp.where(hit, scores, 0.0), axis=0, keepdims=True))
        onehot = jnp.where(hit, 1.0, onehot)
        cand = jnp.where(hit, neg, cand)
    gate = jnp.concatenate(gates, axis=0)
    gate = gate / jnp.sum(gate, axis=0, keepdims=True) * ROUTE_SCALE
    s_ = lax.broadcasted_iota(I32, (tm, tm), 0)
    t_ = lax.broadcasted_iota(I32, (tm, tm), 1)
    before = jnp.where(s_ < t_, 1.0, 0.0).astype(BF16)
    prior = jnp.dot(onehot.astype(BF16), before, preferred_element_type=F32) + run_ref[...]
    ranks = [jnp.sum(jnp.where(eid == a, prior, 0.0), axis=0, keepdims=True) for a in idxs]
    eidx_ref[...] = jnp.concatenate(idxs, axis=0)
    gate_ref[...] = gate
    rank_ref[...] = jnp.concatenate(ranks, axis=0).astype(I32)
    run_ref[...] = run_ref[...] + jnp.sum(onehot, axis=1, keepdims=True)
    cnt_ref[...] = run_ref[...].astype(I32)


def router(h_f32, w_router, bias, tm=256):
    t, d = h_f32.shape
    ne = w_router.shape[1]
    out = lambda dt: jax.ShapeDtypeStruct((TOP_K, t), dt)
    tok = pl.BlockSpec((TOP_K, tm), lambda i: (0, i))
    return pl.pallas_call(
        functools.partial(_router_kernel, ne, N_EXPERT_GROUPS, TOPK_GROUPS, TOP_K),
        grid=(t // tm,),
        in_specs=[pl.BlockSpec((tm, d), lambda i: (i, 0)), pl.BlockSpec((ne, d), lambda i: (0, 0)),
                  pl.BlockSpec((ne, 1), lambda i: (0, 0))],
        out_specs=[tok, tok, tok, pl.BlockSpec((ne, 1), lambda i: (0, 0))],
        out_shape=[out(I32), out(F32), out(I32), jax.ShapeDtypeStruct((ne, 1), I32)],
        scratch_shapes=[pltpu.VMEM((ne, 1), F32)],
        compiler_params=_params(("arbitrary",)),
        name="router",
    )(h_f32, w_router.T.astype(F32), bias.reshape(ne, 1).astype(F32))


def _gather_rows_kernel(rb, idx_ref, src_ref, o_ref, buf_ref, sem):
    def row_copy(r):
        tok = idx_ref[0, 0, r]
        return pltpu.make_async_copy(src_ref.at[pl.ds(tok, 1), :], buf_ref.at[pl.ds(r, 1), :], sem)

    def start(r, carry):
        row_copy(r).start()
        return carry

    def wait(r, carry):
        row_copy(r).wait()
        return carry

    lax.fori_loop(0, rb, start, 0)
    lax.fori_loop(0, rb, wait, 0)
    lo, hi = _unpack_bf16_pair(buf_ref[...])
    o_ref[...] = jnp.concatenate([lo, hi], axis=1).astype(BF16)


def gather_rows(src, row_idx, rb=256):
    n_rows = row_idx.shape[0]
    w = src.shape[1]
    return pl.pallas_call(
        functools.partial(_gather_rows_kernel, rb),
        grid=(n_rows // rb,),
        in_specs=[pl.BlockSpec((1, 1, rb), lambda i: (i, 0, 0), memory_space=pltpu.SMEM),
                  pl.BlockSpec(memory_space=pl.ANY)],
        out_specs=pl.BlockSpec((rb, 2 * w), lambda i: (i, 0)),
        out_shape=jax.ShapeDtypeStruct((n_rows, 2 * w), BF16),
        scratch_shapes=[pltpu.VMEM((rb, w), U32), pltpu.SemaphoreType.DMA(())],
        compiler_params=_params(("arbitrary",)),
        name="moe_dispatch",
    )(row_idx.reshape(n_rows // rb, 1, rb), src)


def _gmm_up_kernel(be_ref, nb_ref, x_ref, wg_ref, wu_ref, o_ref):
    b = pl.program_id(1)

    @pl.when(b < nb_ref[0])
    def _():
        x = x_ref[...]
        g = jnp.dot(x, wg_ref[...].astype(BF16), preferred_element_type=F32)
        u = jnp.dot(x, wu_ref[...].astype(BF16), preferred_element_type=F32)
        o_ref[...] = (_silu(g) * u).astype(o_ref.dtype)

    @pl.when(b >= nb_ref[0])
    def _():
        o_ref[...] = jnp.zeros_like(o_ref)


def gmm_up(x, w_gate, w_up, block_expert, n_used, tm, name, tn=512):
    rows, xw = x.shape
    _, d, f = w_gate.shape
    tn = _tile(f, tn)
    wspec = pl.BlockSpec((None, d, tn), lambda j, b, be, nb: (be[b], 0, j))
    return pl.pallas_call(
        _gmm_up_kernel,
        grid_spec=pltpu.PrefetchScalarGridSpec(
            num_scalar_prefetch=2,
            grid=(f // tn, rows // tm),
            in_specs=[pl.BlockSpec((tm, xw), lambda j, b, be, nb: (b, 0)), wspec, wspec],
            out_specs=pl.BlockSpec((tm, tn), lambda j, b, be, nb: (b, j)),
        ),
        out_shape=jax.ShapeDtypeStruct((rows, f), BF16),
        compiler_params=_params(("arbitrary", "arbitrary")),
        name=name,
    )(block_expert, n_used, x, w_gate, w_up)


def _gmm_down_kernel(be_ref, nb_ref, h_ref, w_ref, o_ref):
    b = pl.program_id(1)

    @pl.when(b < nb_ref[0])
    def _():
        y = jnp.dot(h_ref[...], w_ref[...].astype(BF16), preferred_element_type=F32)
        half = y.shape[1] // 2
        o_ref[...] = _pack_bf16_pair(y[:, :half], y[:, half:])

    @pl.when(b >= nb_ref[0])
    def _():
        o_ref[...] = jnp.zeros_like(o_ref)


def gmm_down(h, w_down, block_expert, n_used, tm, name, n_split=2):
    rows, f = h.shape
    _, _, d = w_down.shape
    tn = d // n_split
    return pl.pallas_call(
        _gmm_down_kernel,
        grid_spec=pltpu.PrefetchScalarGridSpec(
            num_scalar_prefetch=2,
            grid=(n_split, rows // tm),
            in_specs=[pl.BlockSpec((tm, f), lambda j, b, be, nb: (b, 0)),
                      pl.BlockSpec((None, f, tn), lambda j, b, be, nb: (be[b], 0, j))],
            out_specs=pl.BlockSpec((tm, tn // 2), lambda j, b, be, nb: (b, j)),
        ),
        out_shape=jax.ShapeDtypeStruct((rows, d // 2), U32),
        compiler_params=_params(("arbitrary", "arbitrary")),
        name=name,
    )(block_expert, n_used, h, w_down)


def _unpack_rows(w, n_split):
    q = w.shape[1] // n_split
    parts = []
    for s in range(n_split):
        lo, hi = _unpack_bf16_pair(w[:, s * q:(s + 1) * q])
        parts += [lo, hi]
    return jnp.concatenate(parts, axis=1)


def _combine_kernel(topk, n_split, half_blocks, pos_ref, pos_next_ref, y_ref, h_ref, sh_ref, gate_ref, g_ref, b_ref,
                    op_ref, os_ref, buf_ref, sem):
    i = pl.program_id(0)
    n = pl.num_programs(0)
    tb = h_ref.shape[0]
    slot = i % 2

    def row_copy(p_ref, slot_, k, r):
        return pltpu.make_async_copy(y_ref.at[pl.ds(p_ref[0, k, r], 1), :],
                                     buf_ref.at[slot_, k, pl.ds(r, 1), :], sem.at[slot_])

    def issue(p_ref, slot_):
        def body(r, carry):
            for k in range(topk):
                row_copy(p_ref, slot_, k, r).start()
            return carry
        lax.fori_loop(0, tb, body, 0)

    @pl.when(i == 0)
    def _():
        issue(pos_ref, slot)

    @pl.when(i + 1 < n)
    def _():
        issue(pos_next_ref, 1 - slot)

    def wait_body(r, carry):
        for k in range(topk):
            row_copy(pos_ref, slot, k, r).wait()
        return carry

    lax.fori_loop(0, tb, wait_body, 0)

    acc = DEEPNORM_ALPHA * h_ref[...] + _unpack_rows(sh_ref[...], n_split)
    for k in range(topk):
        acc = acc + gate_ref[:, k:k + 1] * _unpack_rows(buf_ref[slot, k], n_split)
    out = _layer_norm(acc, g_ref[...], b_ref[...])

    @pl.when(i < half_blocks)
    def _():
        op_ref[...] = out

    @pl.when(i >= half_blocks)
    def _():
        os_ref[...] = out


def moe_combine(y_sorted, pos, gate, h_f32, shared, g, b, n_first, n_split, tb=64):
    t, d = h_f32.shape
    topk = pos.shape[0]
    nb = t // tb
    hb = n_first // tb
    pos3 = pos.reshape(topk, nb, tb).transpose(1, 0, 2)
    smem = lambda f: pl.BlockSpec((1, topk, tb), f, memory_space=pltpu.SMEM)
    row = lambda w: pl.BlockSpec((tb, w), lambda i: (i, 0))
    vec = pl.BlockSpec((1, d), lambda i: (0, 0))
    return pl.pallas_call(
        functools.partial(_combine_kernel, topk, n_split, hb),
        grid=(nb,),
        in_specs=[smem(lambda i: (i, 0, 0)), smem(lambda i: (jnp.minimum(i + 1, nb - 1), 0, 0)),
                  pl.BlockSpec(memory_space=pl.ANY), row(d), row(d // 2), row(topk), vec, vec],
        out_specs=[pl.BlockSpec((tb, d), lambda i: (jnp.minimum(i, hb - 1), 0)),
                   pl.BlockSpec((tb, d), lambda i: (jnp.maximum(i - hb, 0), 0))],
        out_shape=[jax.ShapeDtypeStruct((n_first, d), F32), jax.ShapeDtypeStruct((t - n_first, d), F32)],
        scratch_shapes=[pltpu.VMEM((2, topk, tb, d // 2), U32), pltpu.SemaphoreType.DMA((2,))],
        compiler_params=_params(("arbitrary",)),
        name="moe_combine",
    )(pos3, pos3, y_sorted, h_f32, shared, gate, g.reshape(1, d), b.reshape(1, d))


MOE_TM = 512


def _moe_schedule(eidx, rank, counts, tm):
    topk, t = eidx.shape
    ne = counts.shape[0]
    n_blocks = (t * topk + ne * (tm - 1) + tm - 1) // tm
    padded = (counts + tm - 1) // tm * tm
    ends = jnp.cumsum(padded)
    starts = ends - padded
    pos = starts[eidx] + rank
    n_used = (ends[-1] // tm).astype(I32)
    blk = jnp.arange(n_blocks, dtype=I32)
    blk_e = jnp.searchsorted(ends, jnp.minimum(blk, n_used - 1) * tm, side="right").astype(I32)
    blk_e = jnp.minimum(blk_e, ne - 1)
    tok = jnp.broadcast_to(jnp.arange(t, dtype=I32)[None, :], (topk, t))
    row_tok = jnp.zeros((n_blocks * tm,), I32).at[pos.reshape(-1)].set(tok.reshape(-1))
    return pos.astype(I32), row_tok, blk_e, n_used.reshape(1)


def kernel(x_prompt, x_sample, mem_prompt, mem_sample, ln_in_g, ln_in_b, w_in, w_dn_conv, dn_a_log, dn_dt_bias,
           dn_norm_w, w_sc_conv, w_mem_kv, w_branch, w_o, ln1_g, ln1_b, w_router, router_bias, w_gate_e, w_up_e,
           w_down_e, w_gate_s, w_up_s, w_down_s, ln2_g, ln2_b):
    assert w_in.shape[0] == DEPTH == 1
    bp, sp, d = x_prompt.shape
    bs, ss, _ = x_sample.shape
    assert bs == 1
    n_a, seq_a = bp * sp, sp
    t = n_a + bs * ss
    nh = dn_a_log.shape[-1]
    hd = dn_norm_w.shape[-1]
    dn_w = nh * hd
    sc_w = w_sc_conv.shape[-1]
    x_w = w_mem_kv.shape[-1] // 2
    n_mem = mem_prompt.shape[1]
    nbr = w_branch.shape[1]
    assert dn_w == sc_w == x_w, "column blocks of the combined projection are addressed in units of one width"
    blk = dn_w
    lyr = 0

    w_all = w_in[lyr]
    ab0 = 4 * dn_w
    w_main = jnp.concatenate([w_all[:, :ab0], w_all[:, ab0 + 4 * nh:]], axis=1).astype(BF16)
    w_ab = w_all[:, ab0:ab0 + 4 * nh].astype(BF16)

    h0_f, h0_b = ln_in(x_prompt.reshape(n_a, d), x_sample.reshape(t - n_a, d), ln_in_g, ln_in_b)
    z = matmul(h0_b, w_main, 1024, 1024, BF16, "in_proj")

    gcol, grow = dn_gates(h0_b, w_ab, dn_a_log[lyr], dn_dt_bias[lyr], nh, DN_CHUNK)
    qkv = dn_prep(z, w_dn_conv[lyr].astype(F32), dn_w, hd, seq_a, n_a)
    o2 = delta_rule(qkv, gcol, grow, nh, hd, seq_a, n_a)
    y_dn = dn_post(o2, z, 3, dn_norm_w[lyr], hd)
    y_sc = short_conv(z, w_sc_conv[lyr].astype(F32), 4, sc_w, seq_a, n_a)
    mem = jnp.concatenate([mem_prompt.reshape(bp * n_mem, d), mem_sample.reshape(bs * n_mem, d)], 0).astype(BF16)
    kv = matmul(mem, w_mem_kv[lyr].astype(BF16), 256, 1024, BF16, "mem_kv")
    xhd = x_w // N_X_HEADS
    y_mem = mem_attention(z, kv, 7 * blk // xhd, N_X_HEADS, xhd, n_mem, seq_a, n_a)

    mixed = gated_merge([y_dn, y_sc, y_mem], z, 8 * blk, w_branch[lyr].astype(BF16))
    attn_out = matmul(mixed, w_o[lyr].astype(BF16), 1024, 1024, F32, "out_proj")
    h_f, h_b, h_pk = res_ln1(h0_f, attn_out, ln1_g[lyr], ln1_b[lyr])

    eidx, gate, rank, counts = router(h_f, w_router[lyr], router_bias[lyr])
    pos, row_tok, blk_e, n_used = _moe_schedule(eidx, rank, counts.reshape(-1), MOE_TM)
    x_sorted = gather_rows(h_pk, row_tok)
    hid = gmm_up(x_sorted, w_gate_e[lyr], w_up_e[lyr], blk_e, n_used, MOE_TM, "moe_up")
    y_sorted = gmm_down(hid, w_down_e[lyr], blk_e, n_used, MOE_TM, "moe_down")
    one = jnp.zeros((t // MOE_TM,), I32)
    all_blocks = jnp.full((1,), t // MOE_TM, I32)
    hid_s = gmm_up(h_b, w_gate_s, w_up_s, one, all_blocks, MOE_TM, "shared_up")
    y_shared = gmm_down(hid_s, w_down_s, one, all_blocks, MOE_TM, "shared_down")
    y_p, y_s = moe_combine(y_sorted, pos, gate.T, h_f, y_shared, ln2_g[lyr], ln2_b[lyr], n_a, 2)
    return y_p.reshape(bp, sp, d), y_s.reshape(bs, ss, d)
```

```python
import functools

import jax
import jax.numpy as jnp
from jax import lax
from jax.experimental import pallas as pl
from jax.experimental.pallas import tpu as pltpu

F32 = jnp.float32
BF16 = jnp.bfloat16
I32 = jnp.int32
U32 = jnp.uint32

DN_CHUNK = 64
DN_HEAD_GROUP = 16
N_X_HEADS = 4
N_EXPERT_GROUPS = 8
TOPK_GROUPS = 4
TOP_K = 8
ROUTE_SCALE = 2.5
DEPTH = 1
DEEPNORM_ALPHA = (2 * DEPTH) ** 0.25
LN_EPS = 1e-5
RMS_EPS = 1e-6
L2_EPS = 1e-6

V7X_VMEM_BYTES = 64 * 1024 * 1024
VMEM_LIMIT = 56 * 1024 * 1024
LANES = 128
HALO = 16


def _params(sem):
    return pltpu.CompilerParams(dimension_semantics=sem, vmem_limit_bytes=VMEM_LIMIT)


def _sigmoid(x):
    return 1.0 / (1.0 + jnp.exp(-x))


def _silu(x):
    return x * _sigmoid(x)


def _softplus(x):
    return jnp.maximum(x, 0.0) + jnp.log1p(jnp.exp(-jnp.abs(x)))


def _tile(n, pref):
    while n % pref:
        pref //= 2
    return pref


def _layer_norm(x, g, b):
    mu = jnp.mean(x, -1, keepdims=True)
    xc = x - mu
    var = jnp.mean(xc * xc, -1, keepdims=True)
    return xc * lax.rsqrt(var + LN_EPS) * g + b


def _pack_bf16_pair(lo, hi):
    lo_b = lax.bitcast_convert_type(lo.astype(BF16).astype(F32), U32)
    hi_b = lax.bitcast_convert_type(hi.astype(BF16).astype(F32), U32)
    return (hi_b & jnp.uint32(0xFFFF0000)) | (lo_b >> 16)


def _unpack_bf16_pair(w):
    lo = lax.bitcast_convert_type(w << 16, F32)
    hi = lax.bitcast_convert_type(w & jnp.uint32(0xFFFF0000), F32)
    return lo, hi


def _ln_in_kernel(half_blocks, xp_ref, xs_ref, g_ref, b_ref, of_ref, ob_ref):
    i = pl.program_id(0)

    def emit(x_ref):
        y = _layer_norm(x_ref[...], g_ref[...], b_ref[...])
        of_ref[...] = y
        ob_ref[...] = y.astype(BF16)

    @pl.when(i < half_blocks)
    def _():
        emit(xp_ref)

    @pl.when(i >= half_blocks)
    def _():
        emit(xs_ref)


def ln_in(xp, xs, g, b, tm=256):
    tp, d = xp.shape
    ts = xs.shape[0]
    hb = tp // tm
    nb = hb + ts // tm
    return pl.pallas_call(
        functools.partial(_ln_in_kernel, hb),
        grid=(nb,),
        in_specs=[
            pl.BlockSpec((tm, d), lambda i: (jnp.minimum(i, hb - 1), 0)),
            pl.BlockSpec((tm, d), lambda i: (jnp.maximum(i - hb, 0), 0)),
            pl.BlockSpec((1, d), lambda i: (0, 0)),
            pl.BlockSpec((1, d), lambda i: (0, 0)),
        ],
        out_specs=[pl.BlockSpec((tm, d), lambda i: (i, 0)), pl.BlockSpec((tm, d), lambda i: (i, 0))],
        out_shape=[jax.ShapeDtypeStruct((tp + ts, d), F32), jax.ShapeDtypeStruct((tp + ts, d), BF16)],
        compiler_params=_params(("arbitrary",)),
        name="ln_in",
    )(xp, xs, g.reshape(1, d), b.reshape(1, d))


def _mm_kernel(a_ref, b_ref, o_ref):
    o_ref[...] = jnp.dot(a_ref[...], b_ref[...], preferred_element_type=F32).astype(o_ref.dtype)


def matmul(a, b, tm, tn, out_dtype, name):
    m, k = a.shape
    n = b.shape[1]
    tm, tn = _tile(m, tm), _tile(n, tn)
    return pl.pallas_call(
        _mm_kernel,
        grid=(m // tm, n // tn),
        in_specs=[pl.BlockSpec((tm, k), lambda i, j: (i, 0)), pl.BlockSpec((k, tn), lambda i, j: (0, j))],
        out_specs=pl.BlockSpec((tm, tn), lambda i, j: (i, j)),
        out_shape=jax.ShapeDtypeStruct((m, n), out_dtype),
        compiler_params=_params(("arbitrary", "arbitrary")),
        name=name,
    )(a, b)


def _dn_gates_kernel(chunk, nh, h_ref, w_ref, wt_ref, alog_ref, dtb_ref, alogt_ref, dtbt_ref, col_ref, row_ref):
    tm = h_ref.shape[0]
    h = h_ref[...]
    zc = jnp.dot(h, w_ref[...], preferred_element_type=F32)
    zr = lax.dot_general(wt_ref[...], h, (((1,), (1,)), ((), ())), preferred_element_type=F32)
    r = lax.broadcasted_iota(I32, (tm, tm), 0)
    c = lax.broadcasted_iota(I32, (tm, tm), 1)
    same = (r // chunk) == (c // chunk)
    le = jnp.where(same & (c <= r), 1.0, 0.0).astype(F32)
    ge = jnp.where(same & (c >= r), 1.0, 0.0).astype(F32)
    hp = lax.Precision.HIGHEST
    for d in range(2):
        a_c = zc[:, d * nh:(d + 1) * nh]
        b_c = zc[:, (2 + d) * nh:(3 + d) * nh]
        g_c = -jnp.exp(alog_ref[:, d * nh:(d + 1) * nh]) * _softplus(a_c + dtb_ref[:, d * nh:(d + 1) * nh])
        a_r = zr[d * nh:(d + 1) * nh, :]
        b_r = zr[(2 + d) * nh:(3 + d) * nh, :]
        g_r = -jnp.exp(alogt_ref[d * nh:(d + 1) * nh, :]) * _softplus(a_r + dtbt_ref[d * nh:(d + 1) * nh, :])
        incl, rest = (le, ge) if d == 0 else (ge, le)
        gc_c = jnp.dot(incl, g_c, preferred_element_type=F32, precision=hp)
        gr_c = jnp.dot(rest, g_c, preferred_element_type=F32, precision=hp) - g_c
        gc_r = jnp.dot(g_r, rest, preferred_element_type=F32, precision=hp)
        gr_r = jnp.dot(g_r, incl, preferred_element_type=F32, precision=hp) - g_r
        col_ref[d] = jnp.concatenate([gc_c, _sigmoid(b_c), gr_c], axis=1)
        row_ref[d] = jnp.concatenate([gc_r, _sigmoid(b_r), gr_r], axis=0)


def dn_gates(h_bf, w_ab, a_log, dt_bias, nh, chunk, tm=256):
    t, d = h_bf.shape
    w_t = w_ab.T
    alog = a_log.reshape(1, 2 * nh).astype(F32)
    dtb = dt_bias.reshape(1, 2 * nh).astype(F32)
    full = lambda shape: pl.BlockSpec(shape, lambda i: (0,) * len(shape))
    return pl.pallas_call(
        functools.partial(_dn_gates_kernel, chunk, nh),
        grid=(t // tm,),
        in_specs=[pl.BlockSpec((tm, d), lambda i: (i, 0)), full((d, 4 * nh)), full((4 * nh, d)),
                  full((1, 2 * nh)), full((1, 2 * nh)), full((2 * nh, 1)), full((2 * nh, 1))],
        out_specs=[pl.BlockSpec((2, tm, 3 * nh), lambda i: (0, i, 0)),
                   pl.BlockSpec((2, 3 * nh, tm), lambda i: (0, 0, i))],
        out_shape=[jax.ShapeDtypeStruct((2, t, 3 * nh), F32), jax.ShapeDtypeStruct((2, 3 * nh, t), F32)],
        compiler_params=_params(("arbitrary",)),
        name="dn_gates",
    )(h_bf, w_ab, w_t, alog, dtb, alog.T, dtb.T)


def _seq_edges(tok0, tm, seq_a, n_a, total):
    end = tok0 + tm
    starts = ((tok0 % seq_a == 0) & (tok0 <= n_a))
    ends = ((end % seq_a == 0) & (end <= n_a)) | (end == total)
    return starts, ends


def _shifted(x, prev_row, next_row):
    tm = x.shape[0]
    rows = lax.broadcasted_iota(I32, x.shape, 0)
    xm = jnp.where(rows == 0, prev_row, pltpu.roll(x, 1, 0))
    xp = jnp.where(rows == tm - 1, next_row, pltpu.roll(x, tm - 1, 0))
    return xm, xp


def _halo_specs(tm, width, col_block, n_rows):
    per = tm // HALO
    last = n_rows // HALO - 1
    return [
        pl.BlockSpec((tm, width), lambda i, *_: (i, col_block(i, *_))),
        pl.BlockSpec((HALO, width), lambda i, *_: (jnp.maximum(i * per - 1, 0), col_block(i, *_))),
        pl.BlockSpec((HALO, width), lambda i, *_: (jnp.minimum((i + 1) * per, last), col_block(i, *_))),
    ]


def _dn_prep_kernel(seq_a, n_a, total, hd, x_ref, xprev_ref, xnext_ref, w_ref, o_ref):
    i = pl.program_id(0)
    j = pl.program_id(1)
    tm, width = x_ref.shape
    starts, ends = _seq_edges(i * tm, tm, seq_a, n_a, total)
    x = x_ref[...].astype(F32)
    prev_row = jnp.where(starts, 0.0, xprev_ref[HALO - 1:HALO, :].astype(F32))
    next_row = jnp.where(ends, 0.0, xnext_ref[0:1, :].astype(F32))
    xm, xp = _shifted(x, prev_row, next_row)
    y = _silu(xm * w_ref[0:1, :] + x * w_ref[1:2, :] + xp * w_ref[2:3, :])

    @pl.when(j == 2)
    def _():
        o_ref[...] = y.astype(o_ref.dtype)

    @pl.when(j < 2)
    def _():
        scale = jnp.where(j == 0, hd ** -0.5, 1.0).astype(F32)
        for h in range(width // hd):
            yh = y[:, h * hd:(h + 1) * hd]
            inv = lax.rsqrt(jnp.sum(yh * yh, -1, keepdims=True) + L2_EPS) * scale
            o_ref[:, h * hd:(h + 1) * hd] = (yh * inv).astype(o_ref.dtype)


def dn_prep(z, w_conv, width, hd, seq_a, n_a, tm=256):
    t = z.shape[0]
    return pl.pallas_call(
        functools.partial(_dn_prep_kernel, seq_a, n_a, t, hd),
        grid=(t // tm, 3),
        in_specs=_halo_specs(tm, width, lambda i, j: j, t) + [pl.BlockSpec((3, width), lambda i, j: (0, j))],
        out_specs=pl.BlockSpec((tm, width), lambda i, j: (i, j)),
        out_shape=jax.ShapeDtypeStruct((t, 3 * width), BF16),
        compiler_params=_params(("arbitrary", "arbitrary")),
        name="dn_prep",
    )(z, z, z, w_conv)


def _delta_kernel(seq_chunks_a, chunks_a, n_chunks, nh, hd,
                  q_ref, k_ref, v_ref, gcol_ref, grow_ref, o_ref, s_ref):
    d = pl.program_id(0)
    i = pl.program_id(1)
    c = jnp.where(d == 0, i, n_chunks - 1 - i)
    cs = q_ref.shape[0]
    first_fwd = ((c % seq_chunks_a == 0) & (c <= chunks_a))
    first_bwd = (((c + 1) % seq_chunks_a == 0) & (c + 1 <= chunks_a)) | (c + 1 == n_chunks)
    first = ((d == 0) & first_fwd) | ((d != 0) & first_bwd)

    @pl.when(first)
    def _():
        s_ref[...] = jnp.zeros_like(s_ref)

    r = lax.broadcasted_iota(I32, (cs, cs), 0)
    cc = lax.broadcasted_iota(I32, (cs, cs), 1)
    ahead = (r - cc) * jnp.where(d == 0, 1, -1)
    tri = ahead >= 0
    strict = ahead > 0
    eye = (r == cc).astype(F32)
    dot = functools.partial(jnp.dot, preferred_element_type=F32)

    nt_dims = (((1,), (1,)), ((), ()))
    tn_dims = (((0,), (0,)), ((), ()))
    n_sq = max(1, (cs - 1).bit_length() - 1)
    outs, states = [], []
    for g0 in range(0, nh, DN_HEAD_GROUP):
        heads = list(range(g0, min(g0 + DN_HEAD_GROUP, nh)))
        cols = [slice(h * hd, (h + 1) * hd) for h in heads]
        q = [q_ref[:, s_] for s_ in cols]
        k = [k_ref[:, s_] for s_ in cols]
        kf = [a.astype(F32) for a in k]
        v = [v_ref[:, s_].astype(F32) for s_ in cols]
        gc = [gcol_ref[:, h:h + 1] for h in heads]
        beta = [gcol_ref[:, nh + h:nh + h + 1] for h in heads]
        gr = [gcol_ref[:, 2 * nh + h:2 * nh + h + 1] for h in heads]
        gc_row = [grow_ref[h:h + 1, :] for h in heads]
        s_old = [s_ref[h] for h in heads]
        sb = [a.astype(BF16) for a in s_old]
        eg = [jnp.exp(a) for a in gc]
        qs = [dot((a.astype(F32) * e).astype(BF16), s_) for a, e, s_ in zip(q, eg, sb)]
        ks = [dot((a * (b * e)).astype(BF16), s_) for a, b, e, s_ in zip(kf, beta, eg, sb)]
        decay = [jnp.exp(jnp.where(tri, a - b, -jnp.inf)) for a, b in zip(gc, gc_row)]
        kq = [lax.dot_general(jnp.concatenate([a, b], axis=0), a, nt_dims, preferred_element_type=F32)
              for a, b in zip(k, q)]
        low = [jnp.where(strict, b * a[:cs] * dcy, 0.0) for a, b, dcy in zip(kq, beta, decay)]
        attn = [(a[cs:] * dcy).astype(BF16) for a, dcy in zip(kq, decay)]
        inv = [eye - a for a in low]
        x = low
        for _ in range(n_sq):
            xb = [a.astype(BF16) for a in x]
            x = [dot(a, a) for a in xb]
            inv = [p + dot(p.astype(BF16), a.astype(BF16)) for p, a in zip(inv, x)]
        v_new = [dot(p.astype(BF16), (a * b - c).astype(BF16)) for p, a, b, c in zip(inv, v, beta, ks)]
        vb = [a.astype(BF16) for a in v_new]
        outs += [a + dot(b, c) for a, b, c in zip(qs, attn, vb)]
        kd = [(a * jnp.exp(b)).astype(BF16) for a, b in zip(kf, gr)]
        g_tot = [jnp.exp(a[0:1, :] + b[0:1, :]) for a, b in zip(gc, gr)]
        states += [s_ * g + lax.dot_general(a, b, tn_dims, preferred_element_type=F32)
                   for s_, g, a, b in zip(s_old, g_tot, kd, vb)]
    o_ref[...] = jnp.concatenate(outs, axis=1)
    for h in range(nh):
        s_ref[h] = states[h]


def delta_rule(qkv, gcol, grow, nh, hd, seq_a, n_a):
    t = qkv.shape[0]
    width = nh * hd
    cs = DN_CHUNK
    n_chunks = t // cs
    ng = gcol.shape[-1]
    grow = grow.reshape(2, ng, n_chunks, cs).transpose(0, 2, 1, 3)
    chunk_of = lambda d, i: jnp.where(d == 0, i, n_chunks - 1 - i)
    return pl.pallas_call(
        functools.partial(_delta_kernel, seq_a // cs, n_a // cs, n_chunks, nh, hd),
        grid=(2, n_chunks),
        in_specs=[
            pl.BlockSpec((cs, width), lambda d, i: (chunk_of(d, i), 0)),
            pl.BlockSpec((cs, width), lambda d, i: (chunk_of(d, i), 1)),
            pl.BlockSpec((cs, width), lambda d, i: (chunk_of(d, i), 2)),
            pl.BlockSpec((None, cs, ng), lambda d, i: (d, chunk_of(d, i), 0)),
            pl.BlockSpec((None, None, ng, cs), lambda d, i: (d, chunk_of(d, i), 0, 0)),
        ],
        out_specs=pl.BlockSpec((None, cs, width), lambda d, i: (d, chunk_of(d, i), 0)),
        out_shape=jax.ShapeDtypeStruct((2, t, width), F32),
        scratch_shapes=[pltpu.VMEM((nh, hd, hd), F32)],
        compiler_params=_params(("arbitrary", "arbitrary")),
        name="delta_rule",
    )(qkv, qkv, qkv, gcol, grow)


def _dn_post_kernel(hd, col_block, o_ref, og_ref, nw_ref, y_ref):
    del col_block
    o = o_ref[0] + o_ref[1]
    og = og_ref[...].astype(F32)
    nw = nw_ref[...]
    for h in range(o.shape[1] // hd):
        hs = slice(h * hd, (h + 1) * hd)
        oh = o[:, hs]
        inv = lax.rsqrt(jnp.mean(oh * oh, -1, keepdims=True) + RMS_EPS)
        y_ref[:, hs] = (oh * inv * nw * _silu(og[:, hs])).astype(y_ref.dtype)


def dn_post(o2, z, og_block, norm_w, hd, tm=256):
    _, t, width = o2.shape
    return pl.pallas_call(
        functools.partial(_dn_post_kernel, hd, og_block),
        grid=(t // tm,),
        in_specs=[pl.BlockSpec((2, tm, width), lambda i: (0, i, 0)),
                  pl.BlockSpec((tm, width), lambda i: (i, og_block)),
                  pl.BlockSpec((1, hd), lambda i: (0, 0))],
        out_specs=pl.BlockSpec((tm, width), lambda i: (i, 0)),
        out_shape=jax.ShapeDtypeStruct((t, width), BF16),
        compiler_params=_params(("arbitrary",)),
        name="dn_post",
    )(o2, z, norm_w.reshape(1, hd).astype(F32))


def _sc_kernel(seq_a, n_a, total, b_ref, c_ref, cprev_ref, cnext_ref, h_ref, hprev_ref, hnext_ref, w_ref, y_ref):
    i = pl.program_id(0)
    tm = b_ref.shape[0]
    starts, ends = _seq_edges(i * tm, tm, seq_a, n_a, total)
    x = c_ref[...].astype(F32) * h_ref[...].astype(F32)
    prev_row = jnp.where(starts, 0.0, cprev_ref[HALO - 1:HALO, :].astype(F32) * hprev_ref[HALO - 1:HALO, :].astype(F32))
    next_row = jnp.where(ends, 0.0, cnext_ref[0:1, :].astype(F32) * hnext_ref[0:1, :].astype(F32))
    xm, xp = _shifted(x, prev_row, next_row)
    conv = xm * w_ref[0:1, :] + x * w_ref[1:2, :] + xp * w_ref[2:3, :]
    y_ref[...] = (b_ref[...].astype(F32) * conv).astype(y_ref.dtype)


def short_conv(z, w_conv, first_block, width, seq_a, n_a, tm=256):
    t = z.shape[0]
    return pl.pallas_call(
        functools.partial(_sc_kernel, seq_a, n_a, t),
        grid=(t // tm,),
        in_specs=([pl.BlockSpec((tm, width), lambda i: (i, first_block))]
                  + _halo_specs(tm, width, lambda i: first_block + 1, t)
                  + _halo_specs(tm, width, lambda i: first_block + 2, t)
                  + [pl.BlockSpec((3, width), lambda i: (0, 0))]),
        out_specs=pl.BlockSpec((tm, width), lambda i: (i, 0)),
        out_shape=jax.ShapeDtypeStruct((t, width), BF16),
        compiler_params=_params(("arbitrary",)),
        name="short_conv",
    )(z, z, z, z, z, z, z, w_conv)


def _xattn_kernel(scale, q_ref, k_ref, v_ref, o_ref):
    s = lax.dot_general(q_ref[...], k_ref[...], (((1,), (1,)), ((), ())), preferred_element_type=F32) * scale
    m = jnp.max(s, -1, keepdims=True)
    e = jnp.exp(s - m)
    p = e / jnp.sum(e, -1, keepdims=True)
    o_ref[...] = jnp.dot(p.astype(BF16), v_ref[...], preferred_element_type=F32).astype(o_ref.dtype)


def mem_attention(z, kv, q_block0, n_heads, hd, n_mem, seq_a, n_a, tm=512):
    t = z.shape[0]
    n_seq_a = n_a // seq_a

    def seq_of(i):
        tok = i * tm
        return jnp.where(tok < n_a, tok // seq_a, n_seq_a)

    return pl.pallas_call(
        functools.partial(_xattn_kernel, hd ** -0.5),
        grid=(t // tm, n_heads),
        in_specs=[pl.BlockSpec((tm, hd), lambda i, h: (i, q_block0 + h)),
                  pl.BlockSpec((n_mem, hd), lambda i, h: (seq_of(i), h)),
                  pl.BlockSpec((n_mem, hd), lambda i, h: (seq_of(i), n_heads + h))],
        out_specs=pl.BlockSpec((tm, hd), lambda i, h: (i, h)),
        out_shape=jax.ShapeDtypeStruct((t, n_heads * hd), BF16),
        compiler_params=_params(("arbitrary", "arbitrary")),
        name="mem_attention",
    )(z, kv, kv)


def _merge_kernel(nb, *refs):
    y_refs = refs[:nb]
    g_refs = refs[nb:2 * nb]
    w_ref = refs[2 * nb]
    o_ref = refs[2 * nb + 1]
    acc = None
    for b in range(nb):
        term = _sigmoid(g_refs[b][...].astype(F32)) * jnp.dot(y_refs[b][...], w_ref[b], preferred_element_type=F32)
        acc = term if acc is None else acc + term
    o_ref[...] = acc.astype(o_ref.dtype)


def gated_merge(ys, z, gate_col0, w_branch, tm=512, tn=512):
    nb = len(ys)
    t, width = ys[0].shape
    d = w_branch.shape[-1]
    g0 = gate_col0 // tn
    per = d // tn
    gate_spec = lambda b: pl.BlockSpec((tm, tn), lambda i, j: (i, g0 + b * per + j))
    return pl.pallas_call(
        functools.partial(_merge_kernel, nb),
        grid=(t // tm, d // tn),
        in_specs=([pl.BlockSpec((tm, width), lambda i, j: (i, 0))] * nb
                  + [gate_spec(b) for b in range(nb)]
                  + [pl.BlockSpec((nb, width, tn), lambda i, j: (0, 0, j))]),
        out_specs=pl.BlockSpec((tm, tn), lambda i, j: (i, j)),
        out_shape=jax.ShapeDtypeStruct((t, d), BF16),
        compiler_params=_params(("arbitrary", "arbitrary")),
        name="gated_merge",
    )(*ys, *([z] * nb), w_branch)


def _res_ln1_kernel(x_ref, y_ref, g_ref, b_ref, of_ref, op_ref):
    h = _layer_norm(DEEPNORM_ALPHA * x_ref[...] + y_ref[...], g_ref[...], b_ref[...])
    of_ref[...] = h
    half = h.shape[1] // 2
    op_ref[...] = _pack_bf16_pair(h[:, :half], h[:, half:])


def res_ln1(x, y, g, b, tm=256):
    t, d = x.shape
    row = lambda w: pl.BlockSpec((tm, w), lambda i: (i, 0))
    vec = pl.BlockSpec((1, d), lambda i: (0, 0))
    return pl.pallas_call(
        _res_ln1_kernel,
        grid=(t // tm,),
        in_specs=[row(d), row(d), vec, vec],
        out_specs=[row(d), row(d // 2)],
        out_shape=[jax.ShapeDtypeStruct((t, d), F32), jax.ShapeDtypeStruct((t, d // 2), U32)],
        compiler_params=_params(("arbitrary",)),
        name="res_ln1",
    )(x, y, g.reshape(1, d), b.reshape(1, d))


def _first_index_of_max(x, idx, big):
    m = jnp.max(x, axis=0, keepdims=True)
    return m, jnp.min(jnp.where(x == m, idx, big), axis=0, keepdims=True)


def _router_kernel(ne, ng, kg, topk, h_ref, wt_ref, bias_ref, eidx_ref, gate_ref, rank_ref, cnt_ref, run_ref):
    i = pl.program_id(0)
    tm = h_ref.shape[0]
    per = ne // ng

    @pl.when(i == 0)
    def _():
        run_ref[...] = jnp.zeros_like(run_ref)

    logits = lax.dot_general(wt_ref[...], h_ref[...], (((1,), (1,)), ((), ())),
                             preferred_element_type=F32, precision=lax.Precision.HIGHEST)
    scores = _sigmoid(logits)
    sel = scores + bias_ref[...]
    eid = lax.broadcasted_iota(I32, (ne, tm), 0)
    neg = -jnp.inf
    pid = lax.broadcasted_iota(I32, (per, tm), 0)
    gscores = []
    for g in range(ng):
        sg = sel[g * per:(g + 1) * per, :]
        m1, a1 = _first_index_of_max(sg, pid, per)
        m2 = jnp.max(jnp.where(pid == a1, neg, sg), axis=0, keepdims=True)
        gscores.append(m1 + m2)
    gscore = jnp.concatenate(gscores, axis=0)
    gid = lax.broadcasted_iota(I32, (ng, tm), 0)
    egroup = eid // per
    keep_e = jnp.zeros((ne, tm), jnp.bool_)
    for _ in range(kg):
        _, a = _first_index_of_max(gscore, gid, ng)
        keep_e = keep_e | (egroup == a)
        gscore = jnp.where(gid == a, neg, gscore)
    cand = jnp.where(keep_e, sel, neg)
    onehot = jnp.zeros((ne, tm), F32)
    idxs, gates = [], []
    for _ in range(topk):
        _, a = _first_index_of_max(cand, eid, ne)
        hit = eid == a
        idxs.append(a)
        gates.append(jnp.sum(jnp.where(hit, scores, 0.0), axis=0, keepdims=True))
        onehot = jnp.where(hit, 1.0, onehot)
        cand = jnp.where(hit, neg, cand)
    gate = jnp.concatenate(gates, axis=0)
    gate = gate / jnp.sum(gate, axis=0, keepdims=True) * ROUTE_SCALE
    s_ = lax.broadcasted_iota(I32, (tm, tm), 0)
    t_ = lax.broadcasted_iota(I32, (tm, tm), 1)
    before = jnp.where(s_ < t_, 1.0, 0.0).astype(BF16)
    prior = jnp.dot(onehot.astype(BF16), before, preferred_element_type=F32) + run_ref[...]
    ranks = [jnp.sum(jnp.where(eid == a, prior, 0.0), axis=0, keepdims=True) for a in idxs]
    eidx_ref[...] = jnp.concatenate(idxs, axis=0)
    gate_ref[...] = gate
    rank_ref[...] = jnp.concatenate(ranks, axis=0).astype(I32)
    run_ref[...] = run_ref[...] + jnp.sum(onehot, axis=1, keepdims=True)
    cnt_ref[...] = run_ref[...].astype(I32)


def router(h_f32, w_router, bias, tm=256):
    t, d = h_f32.shape
    ne = w_router.shape[1]
    out = lambda dt: jax.ShapeDtypeStruct((TOP_K, t), dt)
    tok = pl.BlockSpec((TOP_K, tm), lambda i: (0, i))
    return pl.pallas_call(
        functools.partial(_router_kernel, ne, N_EXPERT_GROUPS, TOPK_GROUPS, TOP_K),
        grid=(t // tm,),
        in_specs=[pl.BlockSpec((tm, d), lambda i: (i, 0)), pl.BlockSpec((ne, d), lambda i: (0, 0)),
                  pl.BlockSpec((ne, 1), lambda i: (0, 0))],
        out_specs=[tok, tok, tok, pl.BlockSpec((ne, 1), lambda i: (0, 0))],
        out_shape=[out(I32), out(F32), out(I32), jax.ShapeDtypeStruct((ne, 1), I32)],
        scratch_shapes=[pltpu.VMEM((ne, 1), F32)],
        compiler_params=_params(("arbitrary",)),
        name="router",
    )(h_f32, w_router.T.astype(F32), bias.reshape(ne, 1).astype(F32))


def _zero_blocks_kernel(blk_ref, o_ref):
    del blk_ref
    o_ref[...] = jnp.zeros_like(o_ref)


def zero_blocks(block_ids, n_rows, w, tm):
    return pl.pallas_call(
        _zero_blocks_kernel,
        grid_spec=pltpu.PrefetchScalarGridSpec(
            num_scalar_prefetch=1,
            grid=(block_ids.shape[0],),
            in_specs=[],
            out_specs=pl.BlockSpec((tm, w), lambda e, blk: (blk[e], 0)),
        ),
        out_shape=jax.ShapeDtypeStruct((n_rows, w), U32),
        compiler_params=_params(("arbitrary",)),
        name="moe_pad_zero",
    )(block_ids)


def _dispatch_kernel(topk, pos_ref, h_ref, dst_in_ref, dst_ref, sem):
    del dst_in_ref
    tb = h_ref.shape[0]

    def issue(r, carry):
        for k in range(topk):
            pltpu.make_async_copy(h_ref.at[pl.ds(r, 1), :], dst_ref.at[pl.ds(pos_ref[0, k, r], 1), :], sem).start()
        return carry

    lax.fori_loop(0, tb, issue, 0, unroll=4)
    for k in range(topk):
        pltpu.make_async_copy(h_ref, dst_ref.at[pl.ds(0, tb), :], sem).wait()


def moe_dispatch(h_pk, pos3, dst, tb):
    t, w = h_pk.shape
    nb, topk, _ = pos3.shape
    return pl.pallas_call(
        functools.partial(_dispatch_kernel, topk),
        grid=(nb,),
        in_specs=[pl.BlockSpec((1, topk, tb), lambda i: (i, 0, 0), memory_space=pltpu.SMEM),
                  pl.BlockSpec((tb, w), lambda i: (i, 0)),
                  pl.BlockSpec(memory_space=pl.ANY)],
        out_specs=pl.BlockSpec(memory_space=pl.ANY),
        out_shape=jax.ShapeDtypeStruct(dst.shape, dst.dtype),
        input_output_aliases={2: 0},
        scratch_shapes=[pltpu.SemaphoreType.DMA(())],
        compiler_params=_params(("arbitrary",)),
        name="moe_dispatch",
    )(pos3, h_pk, dst)


def _gmm_up_kernel(be_ref, nb_ref, x_ref, wg_ref, wu_ref, o_ref):
    b = pl.program_id(1)

    @pl.when(b < nb_ref[0])
    def _():
        lo, hi = _unpack_bf16_pair(x_ref[...])
        x = jnp.concatenate([lo, hi], axis=1).astype(BF16)
        g = jnp.dot(x, wg_ref[...].astype(BF16), preferred_element_type=F32)
        u = jnp.dot(x, wu_ref[...].astype(BF16), preferred_element_type=F32)
        o_ref[...] = (_silu(g) * u).astype(o_ref.dtype)

    @pl.when(b >= nb_ref[0])
    def _():
        o_ref[...] = jnp.zeros_like(o_ref)


def gmm_up(x, w_gate, w_up, block_expert, n_used, tm, name, tn=256):
    rows, xw = x.shape
    _, d, f = w_gate.shape
    tn = _tile(f, tn)
    wspec = pl.BlockSpec((None, d, tn), lambda j, b, be, nb: (be[b], 0, j))
    return pl.pallas_call(
        _gmm_up_kernel,
        grid_spec=pltpu.PrefetchScalarGridSpec(
            num_scalar_prefetch=2,
            grid=(f // tn, rows // tm),
            in_specs=[pl.BlockSpec((tm, xw), lambda j, b, be, nb: (jnp.minimum(b, nb[0] - 1), 0)), wspec, wspec],
            out_specs=pl.BlockSpec((tm, tn), lambda j, b, be, nb: (b, j)),
        ),
        out_shape=jax.ShapeDtypeStruct((rows, f), BF16),
        compiler_params=_params(("arbitrary", "arbitrary")),
        name=name,
    )(block_expert, n_used, x, w_gate, w_up)


def _gmm_down_kernel(be_ref, nb_ref, h_ref, w_ref, o_ref):
    b = pl.program_id(1)

    @pl.when(b < nb_ref[0])
    def _():
        y = jnp.dot(h_ref[...], w_ref[...].astype(BF16), preferred_element_type=F32)
        half = y.shape[1] // 2
        o_ref[...] = _pack_bf16_pair(y[:, :half], y[:, half:])

    @pl.when(b >= nb_ref[0])
    def _():
        o_ref[...] = jnp.zeros_like(o_ref)


def gmm_down(h, w_down, block_expert, n_used, tm, name, n_split=2):
    rows, f = h.shape
    _, _, d = w_down.shape
    tn = d // n_split
    return pl.pallas_call(
        _gmm_down_kernel,
        grid_spec=pltpu.PrefetchScalarGridSpec(
            num_scalar_prefetch=2,
            grid=(n_split, rows // tm),
            in_specs=[pl.BlockSpec((tm, f), lambda j, b, be, nb: (b, 0)),
                      pl.BlockSpec((None, f, tn), lambda j, b, be, nb: (be[b], 0, j))],
            out_specs=pl.BlockSpec((tm, tn // 2), lambda j, b, be, nb: (b, j)),
        ),
        out_shape=jax.ShapeDtypeStruct((rows, d // 2), U32),
        compiler_params=_params(("arbitrary", "arbitrary")),
        name=name,
    )(block_expert, n_used, h, w_down)


def _unpack_rows(w, n_split):
    q = w.shape[1] // n_split
    parts = []
    for s in range(n_split):
        lo, hi = _unpack_bf16_pair(w[:, s * q:(s + 1) * q])
        parts += [lo, hi]
    return jnp.concatenate(parts, axis=1)


def _combine_kernel(topk, n_split, half_blocks, pos_ref, pos_next_ref, y_ref, h_ref, sh_ref, gate_ref, g_ref, b_ref,
                    op_ref, os_ref, buf_ref, sem):
    i = pl.program_id(0)
    n = pl.num_programs(0)
    tb = h_ref.shape[0]
    slot = i % 2

    def row_copy(p_ref, slot_, k, r):
        return pltpu.make_async_copy(y_ref.at[pl.ds(p_ref[0, k, r], 1), :],
                                     buf_ref.at[slot_, k, pl.ds(r, 1), :], sem.at[slot_])

    def issue(p_ref, slot_):
        def body(r, carry):
            for k in range(topk):
                row_copy(p_ref, slot_, k, r).start()
            return carry
        lax.fori_loop(0, tb, body, 0, unroll=True)

    @pl.when(i == 0)
    def _():
        issue(pos_ref, slot)

    @pl.when(i + 1 < n)
    def _():
        issue(pos_next_ref, 1 - slot)

    for k in range(topk):
        pltpu.make_async_copy(y_ref.at[pl.ds(0, tb), :], buf_ref.at[slot, k], sem.at[slot]).wait()

    acc = DEEPNORM_ALPHA * h_ref[...] + _unpack_rows(sh_ref[...], n_split)
    for k in range(topk):
        acc = acc + gate_ref[:, k:k + 1] * _unpack_rows(buf_ref[slot, k], n_split)
    out = _layer_norm(acc, g_ref[...], b_ref[...])

    @pl.when(i < half_blocks)
    def _():
        op_ref[...] = out

    @pl.when(i >= half_blocks)
    def _():
        os_ref[...] = out


def moe_combine(y_sorted, pos3, gate, h_f32, shared, g, b, n_first, n_split):
    t, d = h_f32.shape
    nb, topk, tb = pos3.shape
    hb = n_first // tb
    smem = lambda f: pl.BlockSpec((1, topk, tb), f, memory_space=pltpu.SMEM)
    row = lambda w: pl.BlockSpec((tb, w), lambda i: (i, 0))
    vec = pl.BlockSpec((1, d), lambda i: (0, 0))
    return pl.pallas_call(
        functools.partial(_combine_kernel, topk, n_split, hb),
        grid=(nb,),
        in_specs=[smem(lambda i: (i, 0, 0)), smem(lambda i: (jnp.minimum(i + 1, nb - 1), 0, 0)),
                  pl.BlockSpec(memory_space=pl.ANY), row(d), row(d // 2), row(topk), vec, vec],
        out_specs=[pl.BlockSpec((tb, d), lambda i: (jnp.minimum(i, hb - 1), 0)),
                   pl.BlockSpec((tb, d), lambda i: (jnp.maximum(i - hb, 0), 0))],
        out_shape=[jax.ShapeDtypeStruct((n_first, d), F32), jax.ShapeDtypeStruct((t - n_first, d), F32)],
        scratch_shapes=[pltpu.VMEM((2, topk, tb, d // 2), U32), pltpu.SemaphoreType.DMA((2,))],
        compiler_params=_params(("arbitrary",)),
        name="moe_combine",
    )(pos3, pos3, y_sorted, h_f32, shared, gate, g.reshape(1, d), b.reshape(1, d))


MOE_TM = 512
MOE_TB = 64


def _moe_schedule(eidx, rank, counts, tm, tb):
    topk, t = eidx.shape
    ne = counts.shape[0]
    n_blocks = (t * topk + ne * (tm - 1) + tm - 1) // tm
    padded = (counts + tm - 1) // tm * tm
    ends = jnp.cumsum(padded)
    starts = ends - padded
    start_of = jnp.sum(jnp.where(eidx[:, :, None] == jnp.arange(ne, dtype=I32), starts.astype(I32), 0), -1)
    pos = (start_of + rank).astype(I32)
    pos3 = pos.reshape(topk, t // tb, tb).transpose(1, 0, 2)
    n_used = (ends[-1] // tm).astype(I32)
    blk = jnp.arange(n_blocks, dtype=I32)
    blk_e = jnp.searchsorted(ends, jnp.minimum(blk, n_used - 1) * tm, side="right").astype(I32)
    blk_e = jnp.minimum(blk_e, ne - 1)
    pad_blk = jnp.where(counts > 0, ends // tm - 1, jnp.minimum(starts // tm, n_blocks - 1)).astype(I32)
    return pos3, blk_e, n_used.reshape(1), pad_blk, n_blocks * tm


def kernel(x_prompt, x_sample, mem_prompt, mem_sample, ln_in_g, ln_in_b, w_in, w_dn_conv, dn_a_log, dn_dt_bias,
           dn_norm_w, w_sc_conv, w_mem_kv, w_branch, w_o, ln1_g, ln1_b, w_router, router_bias, w_gate_e, w_up_e,
           w_down_e, w_gate_s, w_up_s, w_down_s, ln2_g, ln2_b):
    assert w_in.shape[0] == DEPTH == 1
    bp, sp, d = x_prompt.shape
    bs, ss, _ = x_sample.shape
    assert bs == 1
    n_a, seq_a = bp * sp, sp
    t = n_a + bs * ss
    nh = dn_a_log.shape[-1]
    hd = dn_norm_w.shape[-1]
    dn_w = nh * hd
    sc_w = w_sc_conv.shape[-1]
    x_w = w_mem_kv.shape[-1] // 2
    n_mem = mem_prompt.shape[1]
    nbr = w_branch.shape[1]
    assert dn_w == sc_w == x_w, "column blocks of the combined projection are addressed in units of one width"
    blk = dn_w
    lyr = 0

    w_all = w_in[lyr]
    ab0 = 4 * dn_w
    w_main = jnp.concatenate([w_all[:, :ab0], w_all[:, ab0 + 4 * nh:]], axis=1).astype(BF16)
    w_ab = w_all[:, ab0:ab0 + 4 * nh].astype(BF16)

    h0_f, h0_b = ln_in(x_prompt.reshape(n_a, d), x_sample.reshape(t - n_a, d), ln_in_g, ln_in_b)
    z = matmul(h0_b, w_main, 1024, 1024, BF16, "in_proj")

    gcol, grow = dn_gates(h0_b, w_ab, dn_a_log[lyr], dn_dt_bias[lyr], nh, DN_CHUNK)
    qkv = dn_prep(z, w_dn_conv[lyr].astype(F32), dn_w, hd, seq_a, n_a)
    o2 = delta_rule(qkv, gcol, grow, nh, hd, seq_a, n_a)
    y_dn = dn_post(o2, z, 3, dn_norm_w[lyr], hd)
    y_sc = short_conv(z, w_sc_conv[lyr].astype(F32), 4, sc_w, seq_a, n_a)
    mem = jnp.concatenate([mem_prompt.reshape(bp * n_mem, d), mem_sample.reshape(bs * n_mem, d)], 0).astype(BF16)
    kv = matmul(mem, w_mem_kv[lyr].astype(BF16), 256, 1024, BF16, "mem_kv")
    xhd = x_w // N_X_HEADS
    y_mem = mem_attention(z, kv, 7 * blk // xhd, N_X_HEADS, xhd, n_mem, seq_a, n_a)

    mixed = gated_merge([y_dn, y_sc, y_mem], z, 8 * blk, w_branch[lyr].astype(BF16))
    attn_out = matmul(mixed, w_o[lyr].astype(BF16), 1024, 1024, F32, "out_proj")
    h_f, h_pk = res_ln1(h0_f, attn_out, ln1_g[lyr], ln1_b[lyr])

    eidx, gate, rank, counts = router(h_f, w_router[lyr], router_bias[lyr])
    pos3, blk_e, n_used, pad_blk, n_rows = _moe_schedule(eidx, rank, counts.reshape(-1), MOE_TM, MOE_TB)
    x_sorted = moe_dispatch(h_pk, pos3, zero_blocks(pad_blk, n_rows, d // 2, MOE_TM), MOE_TB)
    hid = gmm_up(x_sorted, w_gate_e[lyr], w_up_e[lyr], blk_e, n_used, MOE_TM, "moe_up")
    y_sorted = gmm_down(hid, w_down_e[lyr], blk_e, n_used, MOE_TM, "moe_down")
    one = jnp.zeros((t // MOE_TM,), I32)
    all_blocks = jnp.full((1,), t // MOE_TM, I32)
    hid_s = gmm_up(h_pk, w_gate_s, w_up_s, one, all_blocks, MOE_TM, "shared_up")
    y_shared = gmm_down(hid_s, w_down_s, one, all_blocks, MOE_TM, "shared_down")
    y_p, y_s = moe_combine(y_sorted, pos3, gate.T, h_f, y_shared, ln2_g[lyr], ln2_b[lyr], n_a, 2)
    return y_p.reshape(bp, sp, d), y_s.reshape(bs, ss, d)
```

```python
import functools

import jax
import jax.numpy as jnp
from jax import lax
from jax.experimental import pallas as pl
from jax.experimental.pallas import tpu as pltpu

F32 = jnp.float32
BF16 = jnp.bfloat16
I32 = jnp.int32
U32 = jnp.uint32

DN_CHUNK = 64
DN_HEAD_GROUP = 16
N_X_HEADS = 4
N_EXPERT_GROUPS = 8
TOPK_GROUPS = 4
TOP_K = 8
ROUTE_SCALE = 2.5
DEPTH = 1
DEEPNORM_ALPHA = (2 * DEPTH) ** 0.25
LN_EPS = 1e-5
RMS_EPS = 1e-6
L2_EPS = 1e-6

V7X_VMEM_BYTES = 64 * 1024 * 1024
VMEM_LIMIT = 56 * 1024 * 1024
LANES = 128
HALO = 16


def _params(sem):
    return pltpu.CompilerParams(dimension_semantics=sem, vmem_limit_bytes=VMEM_LIMIT)


def _sigmoid(x):
    return 1.0 / (1.0 + jnp.exp(-x))


def _silu(x):
    return x * _sigmoid(x)


def _softplus(x):
    return jnp.maximum(x, 0.0) + jnp.log1p(jnp.exp(-jnp.abs(x)))


def _tile(n, pref):
    while n % pref:
        pref //= 2
    return pref


def _layer_norm(x, g, b):
    mu = jnp.mean(x, -1, keepdims=True)
    xc = x - mu
    var = jnp.mean(xc * xc, -1, keepdims=True)
    return xc * lax.rsqrt(var + LN_EPS) * g + b


def _pack_bf16_pair(lo, hi):
    lo_b = lax.bitcast_convert_type(lo.astype(BF16).astype(F32), U32)
    hi_b = lax.bitcast_convert_type(hi.astype(BF16).astype(F32), U32)
    return (hi_b & jnp.uint32(0xFFFF0000)) | (lo_b >> 16)


def _unpack_bf16_pair(w):
    lo = lax.bitcast_convert_type(w << 16, F32)
    hi = lax.bitcast_convert_type(w & jnp.uint32(0xFFFF0000), F32)
    return lo, hi


def _ln_in_kernel(half_blocks, xp_ref, xs_ref, g_ref, b_ref, of_ref, ob_ref):
    i = pl.program_id(0)

    def emit(x_ref):
        y = _layer_norm(x_ref[...], g_ref[...], b_ref[...])
        of_ref[...] = y
        ob_ref[...] = y.astype(BF16)

    @pl.when(i < half_blocks)
    def _():
        emit(xp_ref)

    @pl.when(i >= half_blocks)
    def _():
        emit(xs_ref)


def ln_in(xp, xs, g, b, tm=256):
    tp, d = xp.shape
    ts = xs.shape[0]
    hb = tp // tm
    nb = hb + ts // tm
    return pl.pallas_call(
        functools.partial(_ln_in_kernel, hb),
        grid=(nb,),
        in_specs=[
            pl.BlockSpec((tm, d), lambda i: (jnp.minimum(i, hb - 1), 0)),
            pl.BlockSpec((tm, d), lambda i: (jnp.maximum(i - hb, 0), 0)),
            pl.BlockSpec((1, d), lambda i: (0, 0)),
            pl.BlockSpec((1, d), lambda i: (0, 0)),
        ],
        out_specs=[pl.BlockSpec((tm, d), lambda i: (i, 0)), pl.BlockSpec((tm, d), lambda i: (i, 0))],
        out_shape=[jax.ShapeDtypeStruct((tp + ts, d), F32), jax.ShapeDtypeStruct((tp + ts, d), BF16)],
        compiler_params=_params(("arbitrary",)),
        name="ln_in",
    )(xp, xs, g.reshape(1, d), b.reshape(1, d))


def _mm_kernel(a_ref, b_ref, o_ref):
    o_ref[...] = jnp.dot(a_ref[...], b_ref[...], preferred_element_type=F32).astype(o_ref.dtype)


def matmul(a, b, tm, tn, out_dtype, name):
    m, k = a.shape
    n = b.shape[1]
    tm, tn = _tile(m, tm), _tile(n, tn)
    return pl.pallas_call(
        _mm_kernel,
        grid=(m // tm, n // tn),
        in_specs=[pl.BlockSpec((tm, k), lambda i, j: (i, 0)), pl.BlockSpec((k, tn), lambda i, j: (0, j))],
        out_specs=pl.BlockSpec((tm, tn), lambda i, j: (i, j)),
        out_shape=jax.ShapeDtypeStruct((m, n), out_dtype),
        compiler_params=_params(("arbitrary", "arbitrary")),
        name=name,
    )(a, b)


def _drop_columns_kernel(n_aligned, gap, a_ref, b_ref, o_ref):
    j = pl.program_id(1)

    @pl.when(j < n_aligned)
    def _():
        o_ref[...] = a_ref[...].astype(o_ref.dtype)

    @pl.when(j >= n_aligned)
    def _():
        o_ref[...] = jnp.concatenate([a_ref[:, gap:], b_ref[:, :gap]], axis=1).astype(o_ref.dtype)


def drop_columns(w, col0, gap, tr=512, tile=1024):
    rows, n_in = w.shape
    n_out = n_in - gap
    tile = _tile(n_out, tile)
    tr = _tile(rows, tr)
    assert col0 % tile == 0 and gap < LANES
    per = tile // LANES
    return pl.pallas_call(
        functools.partial(_drop_columns_kernel, col0 // tile, gap),
        grid=(rows // tr, n_out // tile),
        in_specs=[pl.BlockSpec((tr, tile), lambda i, j: (i, j)),
                  pl.BlockSpec((tr, LANES), lambda i, j: (i, (j + 1) * per))],
        out_specs=pl.BlockSpec((tr, tile), lambda i, j: (i, j)),
        out_shape=jax.ShapeDtypeStruct((rows, n_out), BF16),
        compiler_params=_params(("arbitrary", "arbitrary")),
        name="in_proj_weights",
    )(w, w)


def _dn_gates_kernel(chunk, nh, h_ref, w_ref, wt_ref, alog_ref, dtb_ref, alogt_ref, dtbt_ref, col_ref, row_ref):
    tm = h_ref.shape[0]
    h = h_ref[...]
    zc = jnp.dot(h, w_ref[...], preferred_element_type=F32)
    zr = lax.dot_general(wt_ref[...], h, (((1,), (1,)), ((), ())), preferred_element_type=F32)
    r = lax.broadcasted_iota(I32, (tm, tm), 0)
    c = lax.broadcasted_iota(I32, (tm, tm), 1)
    same = (r // chunk) == (c // chunk)
    le = jnp.where(same & (c <= r), 1.0, 0.0).astype(F32)
    ge = jnp.where(same & (c >= r), 1.0, 0.0).astype(F32)
    hp = lax.Precision.HIGHEST
    for d in range(2):
        a_c = zc[:, d * nh:(d + 1) * nh]
        b_c = zc[:, (2 + d) * nh:(3 + d) * nh]
        g_c = -jnp.exp(alog_ref[:, d * nh:(d + 1) * nh]) * _softplus(a_c + dtb_ref[:, d * nh:(d + 1) * nh])
        a_r = zr[d * nh:(d + 1) * nh, :]
        b_r = zr[(2 + d) * nh:(3 + d) * nh, :]
        g_r = -jnp.exp(alogt_ref[d * nh:(d + 1) * nh, :]) * _softplus(a_r + dtbt_ref[d * nh:(d + 1) * nh, :])
        incl, rest = (le, ge) if d == 0 else (ge, le)
        gc_c = jnp.dot(incl, g_c, preferred_element_type=F32, precision=hp)
        gr_c = jnp.dot(rest, g_c, preferred_element_type=F32, precision=hp) - g_c
        gc_r = jnp.dot(g_r, rest, preferred_element_type=F32, precision=hp)
        gr_r = jnp.dot(g_r, incl, preferred_element_type=F32, precision=hp) - g_r
        col_ref[d] = jnp.concatenate([gc_c, _sigmoid(b_c), gr_c], axis=1)
        row_ref[d] = jnp.concatenate([gc_r, _sigmoid(b_r), gr_r], axis=0)


def dn_gates(h_bf, w_ab, a_log, dt_bias, nh, chunk, tm=256):
    t, d = h_bf.shape
    w_t = w_ab.T
    alog = a_log.reshape(1, 2 * nh).astype(F32)
    dtb = dt_bias.reshape(1, 2 * nh).astype(F32)
    full = lambda shape: pl.BlockSpec(shape, lambda i: (0,) * len(shape))
    return pl.pallas_call(
        functools.partial(_dn_gates_kernel, chunk, nh),
        grid=(t // tm,),
        in_specs=[pl.BlockSpec((tm, d), lambda i: (i, 0)), full((d, 4 * nh)), full((4 * nh, d)),
                  full((1, 2 * nh)), full((1, 2 * nh)), full((2 * nh, 1)), full((2 * nh, 1))],
        out_specs=[pl.BlockSpec((2, tm, 3 * nh), lambda i: (0, i, 0)),
                   pl.BlockSpec((2, 3 * nh, tm), lambda i: (0, 0, i))],
        out_shape=[jax.ShapeDtypeStruct((2, t, 3 * nh), F32), jax.ShapeDtypeStruct((2, 3 * nh, t), F32)],
        compiler_params=_params(("arbitrary",)),
        name="dn_gates",
    )(h_bf, w_ab, w_t, alog, dtb, alog.T, dtb.T)


def _seq_edges(tok0, tm, seq_a, n_a, total):
    end = tok0 + tm
    starts = ((tok0 % seq_a == 0) & (tok0 <= n_a))
    ends = ((end % seq_a == 0) & (end <= n_a)) | (end == total)
    return starts, ends


def _shifted(x, prev_row, next_row):
    tm = x.shape[0]
    rows = lax.broadcasted_iota(I32, x.shape, 0)
    xm = jnp.where(rows == 0, prev_row, pltpu.roll(x, 1, 0))
    xp = jnp.where(rows == tm - 1, next_row, pltpu.roll(x, tm - 1, 0))
    return xm, xp


def _halo_specs(tm, width, col_block, n_rows):
    per = tm // HALO
    last = n_rows // HALO - 1
    return [
        pl.BlockSpec((tm, width), lambda i, *_: (i, col_block(i, *_))),
        pl.BlockSpec((HALO, width), lambda i, *_: (jnp.maximum(i * per - 1, 0), col_block(i, *_))),
        pl.BlockSpec((HALO, width), lambda i, *_: (jnp.minimum((i + 1) * per, last), col_block(i, *_))),
    ]


def _dn_prep_kernel(seq_a, n_a, total, hd, x_ref, xprev_ref, xnext_ref, w_ref, o_ref):
    i = pl.program_id(0)
    j = pl.program_id(1)
    tm, width = x_ref.shape
    starts, ends = _seq_edges(i * tm, tm, seq_a, n_a, total)
    x = x_ref[...].astype(F32)
    prev_row = jnp.where(starts, 0.0, xprev_ref[HALO - 1:HALO, :].astype(F32))
    next_row = jnp.where(ends, 0.0, xnext_ref[0:1, :].astype(F32))
    xm, xp = _shifted(x, prev_row, next_row)
    y = _silu(xm * w_ref[0:1, :] + x * w_ref[1:2, :] + xp * w_ref[2:3, :])

    @pl.when(j == 2)
    def _():
        o_ref[...] = y.astype(o_ref.dtype)

    @pl.when(j < 2)
    def _():
        scale = jnp.where(j == 0, hd ** -0.5, 1.0).astype(F32)
        for h in range(width // hd):
            yh = y[:, h * hd:(h + 1) * hd]
            inv = lax.rsqrt(jnp.sum(yh * yh, -1, keepdims=True) + L2_EPS) * scale
            o_ref[:, h * hd:(h + 1) * hd] = (yh * inv).astype(o_ref.dtype)


def dn_prep(z, w_conv, width, hd, seq_a, n_a, tm=256):
    t = z.shape[0]
    return pl.pallas_call(
        functools.partial(_dn_prep_kernel, seq_a, n_a, t, hd),
        grid=(t // tm, 3),
        in_specs=_halo_specs(tm, width, lambda i, j: j, t) + [pl.BlockSpec((3, width), lambda i, j: (0, j))],
        out_specs=pl.BlockSpec((tm, width), lambda i, j: (i, j)),
        out_shape=jax.ShapeDtypeStruct((t, 3 * width), BF16),
        compiler_params=_params(("arbitrary", "arbitrary")),
        name="dn_prep",
    )(z, z, z, w_conv)


def _delta_kernel(seq_chunks_a, chunks_a, n_chunks, nh, hd,
                  q_ref, k_ref, v_ref, gcol_ref, grow_ref, o_ref, s_ref):
    d = pl.program_id(0)
    i = pl.program_id(1)
    c = jnp.where(d == 0, i, n_chunks - 1 - i)
    cs = q_ref.shape[0]
    first_fwd = ((c % seq_chunks_a == 0) & (c <= chunks_a))
    first_bwd = (((c + 1) % seq_chunks_a == 0) & (c + 1 <= chunks_a)) | (c + 1 == n_chunks)
    first = ((d == 0) & first_fwd) | ((d != 0) & first_bwd)

    @pl.when(first)
    def _():
        s_ref[...] = jnp.zeros_like(s_ref)

    r = lax.broadcasted_iota(I32, (cs, cs), 0)
    cc = lax.broadcasted_iota(I32, (cs, cs), 1)
    ahead = (r - cc) * jnp.where(d == 0, 1, -1)
    tri = ahead >= 0
    strict = ahead > 0
    eye = (r == cc).astype(F32)
    dot = functools.partial(jnp.dot, preferred_element_type=F32)

    nt_dims = (((1,), (1,)), ((), ()))
    tn_dims = (((0,), (0,)), ((), ()))
    n_sq = max(1, (cs - 1).bit_length() - 1)
    outs, states = [], []
    for g0 in range(0, nh, DN_HEAD_GROUP):
        heads = list(range(g0, min(g0 + DN_HEAD_GROUP, nh)))
        cols = [slice(h * hd, (h + 1) * hd) for h in heads]
        q = [q_ref[:, s_] for s_ in cols]
        k = [k_ref[:, s_] for s_ in cols]
        kf = [a.astype(F32) for a in k]
        v = [v_ref[:, s_].astype(F32) for s_ in cols]
        gc = [gcol_ref[:, h:h + 1] for h in heads]
        beta = [gcol_ref[:, nh + h:nh + h + 1] for h in heads]
        gr = [gcol_ref[:, 2 * nh + h:2 * nh + h + 1] for h in heads]
        gc_row = [grow_ref[h:h + 1, :] for h in heads]
        s_old = [s_ref[h] for h in heads]
        sb = [a.astype(BF16) for a in s_old]
        eg = [jnp.exp(a) for a in gc]
        qks = [dot(jnp.concatenate([(a.astype(F32) * e).astype(BF16), (c * (b * e)).astype(BF16)], axis=0), s_)
               for a, c, b, e, s_ in zip(q, kf, beta, eg, sb)]
        qs = [a[:cs] for a in qks]
        ks = [a[cs:] for a in qks]
        decay = [jnp.exp(jnp.where(tri, a - b, -jnp.inf)) for a, b in zip(gc, gc_row)]
        kq = [lax.dot_general(jnp.concatenate([a, b], axis=0), a, nt_dims, preferred_element_type=F32)
              for a, b in zip(k, q)]
        low = [jnp.where(strict, b * a[:cs] * dcy, 0.0) for a, b, dcy in zip(kq, beta, decay)]
        attn = [(a[cs:] * dcy).astype(BF16) for a, dcy in zip(kq, decay)]
        inv = [eye - a for a in low]
        x = low
        for _ in range(n_sq):
            xb = [a.astype(BF16) for a in x]
            x = [dot(a, a) for a in xb]
            inv = [p + dot(p.astype(BF16), a.astype(BF16)) for p, a in zip(inv, x)]
        v_new = [dot(p.astype(BF16), (a * b - c).astype(BF16)) for p, a, b, c in zip(inv, v, beta, ks)]
        vb = [a.astype(BF16) for a in v_new]
        outs += [a + dot(b, c) for a, b, c in zip(qs, attn, vb)]
        kd = [(a * jnp.exp(b)).astype(BF16) for a, b in zip(kf, gr)]
        g_tot = [jnp.exp(a[0:1, :] + b[0:1, :]) for a, b in zip(gc, gr)]
        states += [s_ * g + lax.dot_general(a, b, tn_dims, preferred_element_type=F32)
                   for s_, g, a, b in zip(s_old, g_tot, kd, vb)]
    o_ref[...] = jnp.concatenate(outs, axis=1)
    for h in range(nh):
        s_ref[h] = states[h]


def delta_rule(qkv, gcol, grow, nh, hd, seq_a, n_a):
    t = qkv.shape[0]
    width = nh * hd
    cs = DN_CHUNK
    n_chunks = t // cs
    ng = gcol.shape[-1]
    grow = grow.reshape(2, ng, n_chunks, cs).transpose(0, 2, 1, 3)
    chunk_of = lambda d, i: jnp.where(d == 0, i, n_chunks - 1 - i)
    return pl.pallas_call(
        functools.partial(_delta_kernel, seq_a // cs, n_a // cs, n_chunks, nh, hd),
        grid=(2, n_chunks),
        in_specs=[
            pl.BlockSpec((cs, width), lambda d, i: (chunk_of(d, i), 0)),
            pl.BlockSpec((cs, width), lambda d, i: (chunk_of(d, i), 1)),
            pl.BlockSpec((cs, width), lambda d, i: (chunk_of(d, i), 2)),
            pl.BlockSpec((None, cs, ng), lambda d, i: (d, chunk_of(d, i), 0)),
            pl.BlockSpec((None, None, ng, cs), lambda d, i: (d, chunk_of(d, i), 0, 0)),
        ],
        out_specs=pl.BlockSpec((None, cs, width), lambda d, i: (d, chunk_of(d, i), 0)),
        out_shape=jax.ShapeDtypeStruct((2, t, width), F32),
        scratch_shapes=[pltpu.VMEM((nh, hd, hd), F32)],
        compiler_params=_params(("arbitrary", "arbitrary")),
        name="delta_rule",
    )(qkv, qkv, qkv, gcol, grow)


def _dn_post_kernel(hd, col_block, o_ref, og_ref, nw_ref, y_ref):
    del col_block
    o = o_ref[0] + o_ref[1]
    og = og_ref[...].astype(F32)
    nw = nw_ref[...]
    for h in range(o.shape[1] // hd):
        hs = slice(h * hd, (h + 1) * hd)
        oh = o[:, hs]
        inv = lax.rsqrt(jnp.mean(oh * oh, -1, keepdims=True) + RMS_EPS)
        y_ref[:, hs] = (oh * inv * nw * _silu(og[:, hs])).astype(y_ref.dtype)


def dn_post(o2, z, og_block, norm_w, hd, tm=256):
    _, t, width = o2.shape
    return pl.pallas_call(
        functools.partial(_dn_post_kernel, hd, og_block),
        grid=(t // tm,),
        in_specs=[pl.BlockSpec((2, tm, width), lambda i: (0, i, 0)),
                  pl.BlockSpec((tm, width), lambda i: (i, og_block)),
                  pl.BlockSpec((1, hd), lambda i: (0, 0))],
        out_specs=pl.BlockSpec((tm, width), lambda i: (i, 0)),
        out_shape=jax.ShapeDtypeStruct((t, width), BF16),
        compiler_params=_params(("arbitrary",)),
        name="dn_post",
    )(o2, z, norm_w.reshape(1, hd).astype(F32))


def _sc_kernel(seq_a, n_a, total, b_ref, c_ref, cprev_ref, cnext_ref, h_ref, hprev_ref, hnext_ref, w_ref, y_ref):
    i = pl.program_id(0)
    tm = b_ref.shape[0]
    starts, ends = _seq_edges(i * tm, tm, seq_a, n_a, total)
    x = c_ref[...].astype(F32) * h_ref[...].astype(F32)
    prev_row = jnp.where(starts, 0.0, cprev_ref[HALO - 1:HALO, :].astype(F32) * hprev_ref[HALO - 1:HALO, :].astype(F32))
    next_row = jnp.where(ends, 0.0, cnext_ref[0:1, :].astype(F32) * hnext_ref[0:1, :].astype(F32))
    xm, xp = _shifted(x, prev_row, next_row)
    conv = xm * w_ref[0:1, :] + x * w_ref[1:2, :] + xp * w_ref[2:3, :]
    y_ref[...] = (b_ref[...].astype(F32) * conv).astype(y_ref.dtype)


def short_conv(z, w_conv, first_block, width, seq_a, n_a, tm=256):
    t = z.shape[0]
    return pl.pallas_call(
        functools.partial(_sc_kernel, seq_a, n_a, t),
        grid=(t // tm,),
        in_specs=([pl.BlockSpec((tm, width), lambda i: (i, first_block))]
                  + _halo_specs(tm, width, lambda i: first_block + 1, t)
                  + _halo_specs(tm, width, lambda i: first_block + 2, t)
                  + [pl.BlockSpec((3, width), lambda i: (0, 0))]),
        out_specs=pl.BlockSpec((tm, width), lambda i: (i, 0)),
        out_shape=jax.ShapeDtypeStruct((t, width), BF16),
        compiler_params=_params(("arbitrary",)),
        name="short_conv",
    )(z, z, z, z, z, z, z, w_conv)


def _xattn_kernel(scale, q_ref, k_ref, v_ref, o_ref):
    s = lax.dot_general(q_ref[...], k_ref[...], (((1,), (1,)), ((), ())), preferred_element_type=F32) * scale
    m = jnp.max(s, -1, keepdims=True)
    e = jnp.exp(s - m)
    p = e / jnp.sum(e, -1, keepdims=True)
    o_ref[...] = jnp.dot(p.astype(BF16), v_ref[...], preferred_element_type=F32).astype(o_ref.dtype)


def mem_attention(z, kv, q_block0, n_heads, hd, n_mem, seq_a, n_a, tm=512):
    t = z.shape[0]
    n_seq_a = n_a // seq_a

    def seq_of(i):
        tok = i * tm
        return jnp.where(tok < n_a, tok // seq_a, n_seq_a)

    return pl.pallas_call(
        functools.partial(_xattn_kernel, hd ** -0.5),
        grid=(t // tm, n_heads),
        in_specs=[pl.BlockSpec((tm, hd), lambda i, h: (i, q_block0 + h)),
                  pl.BlockSpec((n_mem, hd), lambda i, h: (seq_of(i), h)),
                  pl.BlockSpec((n_mem, hd), lambda i, h: (seq_of(i), n_heads + h))],
        out_specs=pl.BlockSpec((tm, hd), lambda i, h: (i, h)),
        out_shape=jax.ShapeDtypeStruct((t, n_heads * hd), BF16),
        compiler_params=_params(("arbitrary", "arbitrary")),
        name="mem_attention",
    )(z, kv, kv)


def _merge_kernel(nb, *refs):
    y_refs = refs[:nb]
    g_refs = refs[nb:2 * nb]
    w_ref = refs[2 * nb]
    o_ref = refs[2 * nb + 1]
    acc = None
    for b in range(nb):
        term = _sigmoid(g_refs[b][...].astype(F32)) * jnp.dot(y_refs[b][...], w_ref[b], preferred_element_type=F32)
        acc = term if acc is None else acc + term
    o_ref[...] = acc.astype(o_ref.dtype)


def gated_merge(ys, z, gate_col0, w_branch, tm=512, tn=512):
    nb = len(ys)
    t, width = ys[0].shape
    d = w_branch.shape[-1]
    g0 = gate_col0 // tn
    per = d // tn
    gate_spec = lambda b: pl.BlockSpec((tm, tn), lambda i, j: (i, g0 + b * per + j))
    return pl.pallas_call(
        functools.partial(_merge_kernel, nb),
        grid=(t // tm, d // tn),
        in_specs=([pl.BlockSpec((tm, width), lambda i, j: (i, 0))] * nb
                  + [gate_spec(b) for b in range(nb)]
                  + [pl.BlockSpec((nb, width, tn), lambda i, j: (0, 0, j))]),
        out_specs=pl.BlockSpec((tm, tn), lambda i, j: (i, j)),
        out_shape=jax.ShapeDtypeStruct((t, d), BF16),
        compiler_params=_params(("arbitrary", "arbitrary")),
        name="gated_merge",
    )(*ys, *([z] * nb), w_branch)


def _res_ln1_kernel(x_ref, y_ref, g_ref, b_ref, of_ref, op_ref):
    h = _layer_norm(DEEPNORM_ALPHA * x_ref[...] + y_ref[...], g_ref[...], b_ref[...])
    of_ref[...] = h
    half = h.shape[1] // 2
    op_ref[...] = _pack_bf16_pair(h[:, :half], h[:, half:])


def res_ln1(x, y, g, b, tm=256):
    t, d = x.shape
    row = lambda w: pl.BlockSpec((tm, w), lambda i: (i, 0))
    vec = pl.BlockSpec((1, d), lambda i: (0, 0))
    return pl.pallas_call(
        _res_ln1_kernel,
        grid=(t // tm,),
        in_specs=[row(d), row(d), vec, vec],
        out_specs=[row(d), row(d // 2)],
        out_shape=[jax.ShapeDtypeStruct((t, d), F32), jax.ShapeDtypeStruct((t, d // 2), U32)],
        compiler_params=_params(("arbitrary",)),
        name="res_ln1",
    )(x, y, g.reshape(1, d), b.reshape(1, d))


def _first_index_of_max(x, idx, big):
    m = jnp.max(x, axis=0, keepdims=True)
    return m, jnp.min(jnp.where(x == m, idx, big), axis=0, keepdims=True)


def _router_kernel(ne, ng, kg, topk, h_ref, wt_ref, bias_ref, eidx_ref, gate_ref, rank_ref, cnt_ref, run_ref):
    i = pl.program_id(0)
    tm = h_ref.shape[0]
    per = ne // ng

    @pl.when(i == 0)
    def _():
        run_ref[...] = jnp.zeros_like(run_ref)

    logits = lax.dot_general(wt_ref[...], h_ref[...], (((1,), (1,)), ((), ())),
                             preferred_element_type=F32, precision=lax.Precision.HIGHEST)
    scores = _sigmoid(logits)
    sel = scores + bias_ref[...]
    eid = lax.broadcasted_iota(I32, (ne, tm), 0)
    neg = -jnp.inf
    pid = lax.broadcasted_iota(I32, (per, tm), 0)
    gscores = []
    for g in range(ng):
        sg = sel[g * per:(g + 1) * per, :]
        m1, a1 = _first_index_of_max(sg, pid, per)
        m2 = jnp.max(jnp.where(pid == a1, neg, sg), axis=0, keepdims=True)
        gscores.append(m1 + m2)
    gscore = jnp.concatenate(gscores, axis=0)
    gid = lax.broadcasted_iota(I32, (ng, tm), 0)
    egroup = eid // per
    keep_e = jnp.zeros((ne, tm), jnp.bool_)
    for _ in range(kg):
        _, a = _first_index_of_max(gscore, gid, ng)
        keep_e = keep_e | (egroup == a)
        gscore = jnp.where(gid == a, neg, gscore)
    cand = jnp.where(keep_e, sel, neg)
    onehot = jnp.zeros((ne, tm), F32)
    idxs, gates = [], []
    for _ in range(topk):
        _, a = _first_index_of_max(cand, eid, ne)
        hit = eid == a
        idxs.append(a)
        gates.append(jnp.sum(jnp.where(hit, scores, 0.0), axis=0, keepdims=True))
        onehot = jnp.where(hit, 1.0, onehot)
        cand = jnp.where(hit, neg, cand)
    gate = jnp.concatenate(gates, axis=0)
    gate = gate / jnp.sum(gate, axis=0, keepdims=True) * ROUTE_SCALE
    s_ = lax.broadcasted_iota(I32, (tm, tm), 0)
    t_ = lax.broadcasted_iota(I32, (tm, tm), 1)
    before = jnp.where(s_ < t_, 1.0, 0.0).astype(BF16)
    prior = jnp.dot(onehot.astype(BF16), before, preferred_element_type=F32) + run_ref[...]
    ranks = [jnp.sum(jnp.where(eid == a, prior, 0.0), axis=0, keepdims=True) for a in idxs]
    eidx_ref[...] = jnp.concatenate(idxs, axis=0)
    gate_ref[...] = gate
    rank_ref[...] = jnp.concatenate(ranks, axis=0).astype(I32)
    run_ref[...] = run_ref[...] + jnp.sum(onehot, axis=1, keepdims=True)
    cnt_ref[...] = run_ref[...].astype(I32)


def router(h_f32, w_router, bias, tm=256):
    t, d = h_f32.shape
    ne = w_router.shape[1]
    out = lambda dt: jax.ShapeDtypeStruct((TOP_K, t), dt)
    tok = pl.BlockSpec((TOP_K, tm), lambda i: (0, i))
    return pl.pallas_call(
        functools.partial(_router_kernel, ne, N_EXPERT_GROUPS, TOPK_GROUPS, TOP_K),
        grid=(t // tm,),
        in_specs=[pl.BlockSpec((tm, d), lambda i: (i, 0)), pl.BlockSpec((ne, d), lambda i: (0, 0)),
                  pl.BlockSpec((ne, 1), lambda i: (0, 0))],
        out_specs=[tok, tok, tok, pl.BlockSpec((ne, 1), lambda i: (0, 0))],
        out_shape=[out(I32), out(F32), out(I32), jax.ShapeDtypeStruct((ne, 1), I32)],
        scratch_shapes=[pltpu.VMEM((ne, 1), F32)],
        compiler_params=_params(("arbitrary",)),
        name="router",
    )(h_f32, w_router.T.astype(F32), bias.reshape(ne, 1).astype(F32))


def _zero_blocks_kernel(blk_ref, o_ref):
    del blk_ref
    o_ref[...] = jnp.zeros_like(o_ref)


def zero_blocks(block_ids, n_rows, w, tm):
    return pl.pallas_call(
        _zero_blocks_kernel,
        grid_spec=pltpu.PrefetchScalarGridSpec(
            num_scalar_prefetch=1,
            grid=(block_ids.shape[0],),
            in_specs=[],
            out_specs=pl.BlockSpec((tm, w), lambda e, blk: (blk[e], 0)),
        ),
        out_shape=jax.ShapeDtypeStruct((n_rows, w), U32),
        compiler_params=_params(("arbitrary",)),
        name="moe_pad_zero",
    )(block_ids)


def _dispatch_kernel(topk, pos_ref, h_ref, dst_in_ref, dst_ref, sem):
    del dst_in_ref
    tb = h_ref.shape[0]

    def issue(r, carry):
        for k in range(topk):
            pltpu.make_async_copy(h_ref.at[pl.ds(r, 1), :], dst_ref.at[pl.ds(pos_ref[0, k, r], 1), :], sem).start()
        return carry

    lax.fori_loop(0, tb, issue, 0, unroll=4)
    for k in range(topk):
        pltpu.make_async_copy(h_ref, dst_ref.at[pl.ds(0, tb), :], sem).wait()


def moe_dispatch(h_pk, pos3, dst, tb):
    t, w = h_pk.shape
    nb, topk, _ = pos3.shape
    return pl.pallas_call(
        functools.partial(_dispatch_kernel, topk),
        grid=(nb,),
        in_specs=[pl.BlockSpec((1, topk, tb), lambda i: (i, 0, 0), memory_space=pltpu.SMEM),
                  pl.BlockSpec((tb, w), lambda i: (i, 0)),
                  pl.BlockSpec(memory_space=pl.ANY)],
        out_specs=pl.BlockSpec(memory_space=pl.ANY),
        out_shape=jax.ShapeDtypeStruct(dst.shape, dst.dtype),
        input_output_aliases={2: 0},
        scratch_shapes=[pltpu.SemaphoreType.DMA(())],
        compiler_params=_params(("arbitrary",)),
        name="moe_dispatch",
    )(pos3, h_pk, dst)


GMM_K_CHUNKS = 4


def _gmm_up_kernel(be_ref, nb_ref, x_ref, wg_ref, wu_ref, o_ref):
    b = pl.program_id(1)

    @pl.when(b < nb_ref[0])
    def _():
        words = x_ref.shape[1]
        kc = 2 * words // GMM_K_CHUNKS
        per_half = GMM_K_CHUNKS // 2
        g = u = None
        for c in range(GMM_K_CHUNKS):
            w = x_ref[:, (c % per_half) * kc:(c % per_half + 1) * kc]
            bits = (w << 16) if c < per_half else (w & jnp.uint32(0xFFFF0000))
            xc = lax.bitcast_convert_type(bits, F32).astype(BF16)
            rows = slice(c * kc, (c + 1) * kc)
            gc = jnp.dot(xc, wg_ref[rows, :].astype(BF16), preferred_element_type=F32)
            uc = jnp.dot(xc, wu_ref[rows, :].astype(BF16), preferred_element_type=F32)
            g = gc if g is None else g + gc
            u = uc if u is None else u + uc
        o_ref[...] = (_silu(g) * u).astype(o_ref.dtype)

    @pl.when(b >= nb_ref[0])
    def _():
        o_ref[...] = jnp.zeros_like(o_ref)


def gmm_up(x, w_gate, w_up, block_expert, n_used, tm, name, tn=512):
    rows, xw = x.shape
    _, d, f = w_gate.shape
    tn = _tile(f, tn)
    assert d == 2 * xw and (d // GMM_K_CHUNKS) % LANES == 0
    wspec = pl.BlockSpec((None, d, tn), lambda j, b, be, nb: (be[b], 0, j))
    return pl.pallas_call(
        _gmm_up_kernel,
        grid_spec=pltpu.PrefetchScalarGridSpec(
            num_scalar_prefetch=2,
            grid=(f // tn, rows // tm),
            in_specs=[pl.BlockSpec((tm, xw), lambda j, b, be, nb: (jnp.minimum(b, nb[0] - 1), 0)), wspec, wspec],
            out_specs=pl.BlockSpec((tm, tn), lambda j, b, be, nb: (b, j)),
        ),
        out_shape=jax.ShapeDtypeStruct((rows, f), BF16),
        compiler_params=_params(("arbitrary", "arbitrary")),
        name=name,
    )(block_expert, n_used, x, w_gate, w_up)


def _gmm_down_kernel(n_ranges, be_ref, nb_ref, h_ref, w_ref, o_ref):
    b = pl.program_id(1)

    @pl.when(b < nb_ref[0])
    def _():
        h = h_ref[...]
        cw = w_ref.shape[1] // n_ranges
        for c in range(n_ranges):
            y = jnp.dot(h, w_ref[:, c * cw:(c + 1) * cw].astype(BF16), preferred_element_type=F32)
            o_ref[:, c * (cw // 2):(c + 1) * (cw // 2)] = _pack_bf16_pair(y[:, :cw // 2], y[:, cw // 2:])

    @pl.when(b >= nb_ref[0])
    def _():
        o_ref[...] = jnp.zeros_like(o_ref)


def down_ranges(d):
    return max(r for r in (4, 2, 1) if (d // r) % (2 * LANES) == 0)


def gmm_down(h, w_down, block_expert, n_used, tm, name):
    rows, f = h.shape
    _, _, d = w_down.shape
    return pl.pallas_call(
        functools.partial(_gmm_down_kernel, down_ranges(d)),
        grid_spec=pltpu.PrefetchScalarGridSpec(
            num_scalar_prefetch=2,
            grid=(1, rows // tm),
            in_specs=[pl.BlockSpec((tm, f), lambda j, b, be, nb: (jnp.minimum(b, nb[0] - 1), 0)),
                      pl.BlockSpec((None, f, d), lambda j, b, be, nb: (be[b], 0, 0))],
            out_specs=pl.BlockSpec((tm, d // 2), lambda j, b, be, nb: (b, 0)),
        ),
        out_shape=jax.ShapeDtypeStruct((rows, d // 2), U32),
        compiler_params=_params(("arbitrary", "arbitrary")),
        name=name,
    )(block_expert, n_used, h, w_down)


def _unpack_rows(w, n_split):
    q = w.shape[1] // n_split
    parts = []
    for s in range(n_split):
        lo, hi = _unpack_bf16_pair(w[:, s * q:(s + 1) * q])
        parts += [lo, hi]
    return jnp.concatenate(parts, axis=1)


COMBINE_DEPTH = 3


def _combine_kernel(topk, n_split, half_blocks, pos_ref, pos1_ref, pos2_ref, y_ref, h_ref, sh_ref, gate_ref, g_ref,
                    b_ref, op_ref, os_ref, *scratch):
    bufs, sem = scratch[:COMBINE_DEPTH], scratch[COMBINE_DEPTH]
    i = pl.program_id(0)
    n = pl.num_programs(0)
    tb = h_ref.shape[0]

    def issue(p_ref, s):
        def body(r, carry):
            for k in range(topk):
                pltpu.make_async_copy(y_ref.at[pl.ds(p_ref[0, k, r], 1), :], bufs[s].at[k, pl.ds(r, 1), :],
                                      sem.at[s]).start()
            return carry
        lax.fori_loop(0, tb, body, 0, unroll=True)

    def drain(s):
        for k in range(topk):
            pltpu.make_async_copy(y_ref.at[pl.ds(0, tb), :], bufs[s].at[k], sem.at[s]).wait()

    @pl.when(i == 0)
    def _():
        issue(pos_ref, 0)
        issue(pos1_ref, 1)

    def step(s):
        drain(s)
        issue(pos2_ref, (s + 2) % COMBINE_DEPTH)
        acc = DEEPNORM_ALPHA * h_ref[...] + _unpack_rows(sh_ref[...], n_split)
        for k in range(topk):
            acc = acc + gate_ref[:, k:k + 1] * _unpack_rows(bufs[s][k], n_split)
        out = _layer_norm(acc, g_ref[...], b_ref[...])

        @pl.when(i < half_blocks)
        def _():
            op_ref[...] = out

        @pl.when(i >= half_blocks)
        def _():
            os_ref[...] = out

        @pl.when(i == n - 1)
        def _():
            drain((s + 1) % COMBINE_DEPTH)
            drain((s + 2) % COMBINE_DEPTH)

    for s in range(COMBINE_DEPTH):
        pl.when(i % COMBINE_DEPTH == s)(functools.partial(step, s))


def moe_combine(y_sorted, pos3, gate, h_f32, shared, g, b, n_first, n_split):
    t, d = h_f32.shape
    nb, topk, tb = pos3.shape
    hb = n_first // tb
    smem = lambda f: pl.BlockSpec((1, topk, tb), f, memory_space=pltpu.SMEM)
    row = lambda w: pl.BlockSpec((tb, w), lambda i: (i, 0))
    vec = pl.BlockSpec((1, d), lambda i: (0, 0))
    return pl.pallas_call(
        functools.partial(_combine_kernel, topk, n_split, hb),
        grid=(nb,),
        in_specs=[smem(lambda i: (i, 0, 0)), smem(lambda i: (jnp.minimum(i + 1, nb - 1), 0, 0)),
                  smem(lambda i: (jnp.minimum(i + 2, nb - 1), 0, 0)),
                  pl.BlockSpec(memory_space=pl.ANY), row(d), row(d // 2), row(topk), vec, vec],
        out_specs=[pl.BlockSpec((tb, d), lambda i: (jnp.minimum(i, hb - 1), 0)),
                   pl.BlockSpec((tb, d), lambda i: (jnp.maximum(i - hb, 0), 0))],
        out_shape=[jax.ShapeDtypeStruct((n_first, d), F32), jax.ShapeDtypeStruct((t - n_first, d), F32)],
        scratch_shapes=([pltpu.VMEM((topk, tb, d // 2), U32)] * COMBINE_DEPTH
                        + [pltpu.SemaphoreType.DMA((COMBINE_DEPTH,))]),
        compiler_params=_params(("arbitrary",)),
        name="moe_combine",
    )(pos3, pos3, pos3, y_sorted, h_f32, shared, gate, g.reshape(1, d), b.reshape(1, d))


MOE_TM = 512
MOE_TB = 64


def _moe_schedule(eidx, rank, counts, tm, tb):
    topk, t = eidx.shape
    ne = counts.shape[0]
    n_blocks = (t * topk + ne * (tm - 1) + tm - 1) // tm
    padded = (counts + tm - 1) // tm * tm
    ends = jnp.cumsum(padded)
    starts = ends - padded
    start_of = jnp.sum(jnp.where(eidx[:, :, None] == jnp.arange(ne, dtype=I32), starts.astype(I32), 0), -1)
    pos = (start_of + rank).astype(I32)
    pos3 = pos.reshape(topk, t // tb, tb).transpose(1, 0, 2)
    n_used = (ends[-1] // tm).astype(I32)
    blk = jnp.arange(n_blocks, dtype=I32)
    blk_e = jnp.searchsorted(ends, jnp.minimum(blk, n_used - 1) * tm, side="right").astype(I32)
    blk_e = jnp.minimum(blk_e, ne - 1)
    pad_blk = jnp.where(counts > 0, ends // tm - 1, jnp.minimum(starts // tm, n_blocks - 1)).astype(I32)
    return pos3, blk_e, n_used.reshape(1), pad_blk, n_blocks * tm


def kernel(x_prompt, x_sample, mem_prompt, mem_sample, ln_in_g, ln_in_b, w_in, w_dn_conv, dn_a_log, dn_dt_bias,
           dn_norm_w, w_sc_conv, w_mem_kv, w_branch, w_o, ln1_g, ln1_b, w_router, router_bias, w_gate_e, w_up_e,
           w_down_e, w_gate_s, w_up_s, w_down_s, ln2_g, ln2_b):
    assert w_in.shape[0] == DEPTH == 1
    bp, sp, d = x_prompt.shape
    bs, ss, _ = x_sample.shape
    assert bs == 1
    n_a, seq_a = bp * sp, sp
    t = n_a + bs * ss
    nh = dn_a_log.shape[-1]
    hd = dn_norm_w.shape[-1]
    dn_w = nh * hd
    sc_w = w_sc_conv.shape[-1]
    x_w = w_mem_kv.shape[-1] // 2
    n_mem = mem_prompt.shape[1]
    nbr = w_branch.shape[1]
    assert dn_w == sc_w == x_w, "column blocks of the combined projection are addressed in units of one width"
    blk = dn_w
    lyr = 0

    w_all = w_in[lyr]
    ab0 = 4 * dn_w
    w_main = drop_columns(w_all, ab0, 4 * nh)
    w_ab = w_all[:, ab0:ab0 + 4 * nh].astype(BF16)

    h0_f, h0_b = ln_in(x_prompt.reshape(n_a, d), x_sample.reshape(t - n_a, d), ln_in_g, ln_in_b)
    z = matmul(h0_b, w_main, 1024, 1024, BF16, "in_proj")

    gcol, grow = dn_gates(h0_b, w_ab, dn_a_log[lyr], dn_dt_bias[lyr], nh, DN_CHUNK)
    qkv = dn_prep(z, w_dn_conv[lyr].astype(F32), dn_w, hd, seq_a, n_a)
    o2 = delta_rule(qkv, gcol, grow, nh, hd, seq_a, n_a)
    y_dn = dn_post(o2, z, 3, dn_norm_w[lyr], hd)
    y_sc = short_conv(z, w_sc_conv[lyr].astype(F32), 4, sc_w, seq_a, n_a)
    mem = jnp.concatenate([mem_prompt.reshape(bp * n_mem, d), mem_sample.reshape(bs * n_mem, d)], 0).astype(BF16)
    kv = matmul(mem, w_mem_kv[lyr].astype(BF16), 256, 1024, BF16, "mem_kv")
    xhd = x_w // N_X_HEADS
    y_mem = mem_attention(z, kv, 7 * blk // xhd, N_X_HEADS, xhd, n_mem, seq_a, n_a)

    mixed = gated_merge([y_dn, y_sc, y_mem], z, 8 * blk, w_branch[lyr].astype(BF16))
    attn_out = matmul(mixed, w_o[lyr].astype(BF16), 1024, 1024, F32, "out_proj")
    h_f, h_pk = res_ln1(h0_f, attn_out, ln1_g[lyr], ln1_b[lyr])

    eidx, gate, rank, counts = router(h_f, w_router[lyr], router_bias[lyr])
    pos3, blk_e, n_used, pad_blk, n_rows = _moe_schedule(eidx, rank, counts.reshape(-1), MOE_TM, MOE_TB)
    x_sorted = moe_dispatch(h_pk, pos3, zero_blocks(pad_blk, n_rows, d // 2, MOE_TM), MOE_TB)
    hid = gmm_up(x_sorted, w_gate_e[lyr], w_up_e[lyr], blk_e, n_used, MOE_TM, "moe_up")
    y_sorted = gmm_down(hid, w_down_e[lyr], blk_e, n_used, MOE_TM, "moe_down")
    one = jnp.zeros((t // MOE_TM,), I32)
    all_blocks = jnp.full((1,), t // MOE_TM, I32)
    hid_s = gmm_up(h_pk, w_gate_s, w_up_s, one, all_blocks, MOE_TM, "shared_up")
    y_shared = gmm_down(hid_s, w_down_s, one, all_blocks, MOE_TM, "shared_down")
    y_p, y_s = moe_combine(y_sorted, pos3, gate.T, h_f, y_shared, ln2_g[lyr], ln2_b[lyr], n_a, down_ranges(d))
    return y_p.reshape(bp, sp, d), y_s.reshape(bs, ss, d)
```

```python
import functools

import jax
import jax.numpy as jnp
from jax import lax
from jax.experimental import pallas as pl
from jax.experimental.pallas import tpu as pltpu

F32 = jnp.float32
BF16 = jnp.bfloat16
I32 = jnp.int32
U32 = jnp.uint32

DN_CHUNK = 64
N_X_HEADS = 4
N_EXPERT_GROUPS = 8
TOPK_GROUPS = 4
TOP_K = 8
ROUTE_SCALE = 2.5
DEPTH = 1
DEEPNORM_ALPHA = (2 * DEPTH) ** 0.25
LN_EPS = 1e-5
RMS_EPS = 1e-6
L2_EPS = 1e-6

V7X_VMEM_BYTES = 64 * 1024 * 1024
VMEM_LIMIT = 56 * 1024 * 1024
LANES = 128
HALO = 16


def _params(sem):
    return pltpu.CompilerParams(dimension_semantics=sem, vmem_limit_bytes=VMEM_LIMIT)


def _sigmoid(x):
    return 1.0 / (1.0 + jnp.exp(-x))


def _silu(x):
    return x * _sigmoid(x)


def _softplus(x):
    return jnp.maximum(x, 0.0) + jnp.log1p(jnp.exp(-jnp.abs(x)))


def _tile(n, pref):
    while n % pref:
        pref //= 2
    return pref


def _layer_norm(x, g, b):
    mu = jnp.mean(x, -1, keepdims=True)
    xc = x - mu
    var = jnp.mean(xc * xc, -1, keepdims=True)
    return xc * lax.rsqrt(var + LN_EPS) * g + b


def _pack_bf16_pair(lo, hi):
    lo_b = lax.bitcast_convert_type(lo.astype(BF16).astype(F32), U32)
    hi_b = lax.bitcast_convert_type(hi.astype(BF16).astype(F32), U32)
    return (hi_b & jnp.uint32(0xFFFF0000)) | (lo_b >> 16)


def _unpack_bf16_pair(w):
    lo = lax.bitcast_convert_type(w << 16, F32)
    hi = lax.bitcast_convert_type(w & jnp.uint32(0xFFFF0000), F32)
    return lo, hi


def _ln_in_kernel(half_blocks, xp_ref, xs_ref, g_ref, b_ref, of_ref, ob_ref):
    i = pl.program_id(0)

    def emit(x_ref):
        y = _layer_norm(x_ref[...], g_ref[...], b_ref[...])
        of_ref[...] = y
        ob_ref[...] = y.astype(BF16)

    @pl.when(i < half_blocks)
    def _():
        emit(xp_ref)

    @pl.when(i >= half_blocks)
    def _():
        emit(xs_ref)


def ln_in(xp, xs, g, b, tm=256):
    tp, d = xp.shape
    ts = xs.shape[0]
    hb = tp // tm
    nb = hb + ts // tm
    return pl.pallas_call(
        functools.partial(_ln_in_kernel, hb),
        grid=(nb,),
        in_specs=[
            pl.BlockSpec((tm, d), lambda i: (jnp.minimum(i, hb - 1), 0)),
            pl.BlockSpec((tm, d), lambda i: (jnp.maximum(i - hb, 0), 0)),
            pl.BlockSpec((1, d), lambda i: (0, 0)),
            pl.BlockSpec((1, d), lambda i: (0, 0)),
        ],
        out_specs=[pl.BlockSpec((tm, d), lambda i: (i, 0)), pl.BlockSpec((tm, d), lambda i: (i, 0))],
        out_shape=[jax.ShapeDtypeStruct((tp + ts, d), F32), jax.ShapeDtypeStruct((tp + ts, d), BF16)],
        compiler_params=_params(("arbitrary",)),
        name="ln_in",
    )(xp, xs, g.reshape(1, d), b.reshape(1, d))


def _mm_kernel(a_ref, b_ref, o_ref):
    o_ref[...] = jnp.dot(a_ref[...], b_ref[...], preferred_element_type=F32).astype(o_ref.dtype)


def matmul(a, b, tm, tn, out_dtype, name):
    m, k = a.shape
    n = b.shape[1]
    tm, tn = _tile(m, tm), _tile(n, tn)
    return pl.pallas_call(
        _mm_kernel,
        grid=(m // tm, n // tn),
        in_specs=[pl.BlockSpec((tm, k), lambda i, j: (i, 0)), pl.BlockSpec((k, tn), lambda i, j: (0, j))],
        out_specs=pl.BlockSpec((tm, tn), lambda i, j: (i, j)),
        out_shape=jax.ShapeDtypeStruct((m, n), out_dtype),
        compiler_params=_params(("arbitrary", "arbitrary")),
        name=name,
    )(a, b)


def _drop_columns_kernel(n_aligned, gap, a_ref, b_ref, o_ref):
    j = pl.program_id(1)

    @pl.when(j < n_aligned)
    def _():
        o_ref[...] = a_ref[...].astype(o_ref.dtype)

    @pl.when(j >= n_aligned)
    def _():
        o_ref[...] = jnp.concatenate([a_ref[:, gap:], b_ref[:, :gap]], axis=1).astype(o_ref.dtype)


def drop_columns(w, layer, col0, gap, tr=512, tile=1024):
    _, rows, n_in = w.shape
    n_out = n_in - gap
    tile = _tile(n_out, tile)
    tr = _tile(rows, tr)
    assert col0 % tile == 0 and gap < LANES
    per = tile // LANES
    return pl.pallas_call(
        functools.partial(_drop_columns_kernel, col0 // tile, gap),
        grid=(rows // tr, n_out // tile),
        in_specs=[pl.BlockSpec((None, tr, tile), lambda i, j: (layer, i, j)),
                  pl.BlockSpec((None, tr, LANES), lambda i, j: (layer, i, (j + 1) * per))],
        out_specs=pl.BlockSpec((tr, tile), lambda i, j: (i, j)),
        out_shape=jax.ShapeDtypeStruct((rows, n_out), BF16),
        compiler_params=_params(("arbitrary", "arbitrary")),
        name="in_proj_weights",
    )(w, w)


def _dn_gates_kernel(chunk, nh, h_ref, w_ref, wt_ref, alog_ref, dtb_ref, alogt_ref, dtbt_ref, col_ref, row_ref):
    tm = h_ref.shape[0]
    h = h_ref[...]
    zc = jnp.dot(h, w_ref[...], preferred_element_type=F32)
    zr = lax.dot_general(wt_ref[...], h, (((1,), (1,)), ((), ())), preferred_element_type=F32)
    r = lax.broadcasted_iota(I32, (tm, tm), 0)
    c = lax.broadcasted_iota(I32, (tm, tm), 1)
    same = (r // chunk) == (c // chunk)
    le = jnp.where(same & (c <= r), 1.0, 0.0).astype(F32)
    ge = jnp.where(same & (c >= r), 1.0, 0.0).astype(F32)
    hp = lax.Precision.HIGHEST
    for d in range(2):
        a_c = zc[:, d * nh:(d + 1) * nh]
        b_c = zc[:, (2 + d) * nh:(3 + d) * nh]
        g_c = -jnp.exp(alog_ref[:, d * nh:(d + 1) * nh]) * _softplus(a_c + dtb_ref[:, d * nh:(d + 1) * nh])
        a_r = zr[d * nh:(d + 1) * nh, :]
        b_r = zr[(2 + d) * nh:(3 + d) * nh, :]
        g_r = -jnp.exp(alogt_ref[d * nh:(d + 1) * nh, :]) * _softplus(a_r + dtbt_ref[d * nh:(d + 1) * nh, :])
        incl, rest = (le, ge) if d == 0 else (ge, le)
        gc_c = jnp.dot(incl, g_c, preferred_element_type=F32, precision=hp)
        gr_c = jnp.dot(rest, g_c, preferred_element_type=F32, precision=hp) - g_c
        gc_r = jnp.dot(g_r, rest, preferred_element_type=F32, precision=hp)
        gr_r = jnp.dot(g_r, incl, preferred_element_type=F32, precision=hp) - g_r
        col_ref[d] = jnp.concatenate([gc_c, _sigmoid(b_c), gr_c], axis=1)
        row_ref[d] = jnp.concatenate([gc_r, _sigmoid(b_r), gr_r], axis=0)


def dn_gates(h_bf, w_ab, a_log, dt_bias, nh, chunk, tm=256):
    t, d = h_bf.shape
    w_t = w_ab.T
    alog = a_log.reshape(1, 2 * nh).astype(F32)
    dtb = dt_bias.reshape(1, 2 * nh).astype(F32)
    full = lambda shape: pl.BlockSpec(shape, lambda i: (0,) * len(shape))
    return pl.pallas_call(
        functools.partial(_dn_gates_kernel, chunk, nh),
        grid=(t // tm,),
        in_specs=[pl.BlockSpec((tm, d), lambda i: (i, 0)), full((d, 4 * nh)), full((4 * nh, d)),
                  full((1, 2 * nh)), full((1, 2 * nh)), full((2 * nh, 1)), full((2 * nh, 1))],
        out_specs=[pl.BlockSpec((2, tm, 3 * nh), lambda i: (0, i, 0)),
                   pl.BlockSpec((2, 3 * nh, tm), lambda i: (0, 0, i))],
        out_shape=[jax.ShapeDtypeStruct((2, t, 3 * nh), F32), jax.ShapeDtypeStruct((2, 3 * nh, t), F32)],
        compiler_params=_params(("arbitrary",)),
        name="dn_gates",
    )(h_bf, w_ab, w_t, alog, dtb, alog.T, dtb.T)


def _seq_edges(tok0, tm, seq_a, n_a, total):
    end = tok0 + tm
    starts = ((tok0 % seq_a == 0) & (tok0 <= n_a))
    ends = ((end % seq_a == 0) & (end <= n_a)) | (end == total)
    return starts, ends


def _shifted(x, prev_row, next_row):
    tm = x.shape[0]
    rows = lax.broadcasted_iota(I32, x.shape, 0)
    xm = jnp.where(rows == 0, prev_row, pltpu.roll(x, 1, 0))
    xp = jnp.where(rows == tm - 1, next_row, pltpu.roll(x, tm - 1, 0))
    return xm, xp


def _halo_specs(tm, width, col_block, n_rows):
    per = tm // HALO
    last = n_rows // HALO - 1
    return [
        pl.BlockSpec((tm, width), lambda i, *_: (i, col_block(i, *_))),
        pl.BlockSpec((HALO, width), lambda i, *_: (jnp.maximum(i * per - 1, 0), col_block(i, *_))),
        pl.BlockSpec((HALO, width), lambda i, *_: (jnp.minimum((i + 1) * per, last), col_block(i, *_))),
    ]


def _dn_prep_kernel(seq_a, n_a, total, hd, x_ref, xprev_ref, xnext_ref, w_ref, o_ref):
    i = pl.program_id(0)
    j = pl.program_id(1)
    tm, width = x_ref.shape
    starts, ends = _seq_edges(i * tm, tm, seq_a, n_a, total)
    x = x_ref[...].astype(F32)
    prev_row = jnp.where(starts, 0.0, xprev_ref[HALO - 1:HALO, :].astype(F32))
    next_row = jnp.where(ends, 0.0, xnext_ref[0:1, :].astype(F32))
    xm, xp = _shifted(x, prev_row, next_row)
    y = _silu(xm * w_ref[0:1, :] + x * w_ref[1:2, :] + xp * w_ref[2:3, :])

    @pl.when(j == 2)
    def _():
        o_ref[...] = y.astype(o_ref.dtype)

    @pl.when(j < 2)
    def _():
        scale = jnp.where(j == 0, hd ** -0.5, 1.0).astype(F32)
        for h in range(width // hd):
            yh = y[:, h * hd:(h + 1) * hd]
            inv = lax.rsqrt(jnp.sum(yh * yh, -1, keepdims=True) + L2_EPS) * scale
            o_ref[:, h * hd:(h + 1) * hd] = (yh * inv).astype(o_ref.dtype)


def dn_prep(z, w_conv, width, hd, seq_a, n_a, tm=256):
    t = z.shape[0]
    return pl.pallas_call(
        functools.partial(_dn_prep_kernel, seq_a, n_a, t, hd),
        grid=(t // tm, 3),
        in_specs=_halo_specs(tm, width, lambda i, j: j, t) + [pl.BlockSpec((3, width), lambda i, j: (0, j))],
        out_specs=pl.BlockSpec((tm, width), lambda i, j: (i, j)),
        out_shape=jax.ShapeDtypeStruct((t, 3 * width), BF16),
        compiler_params=_params(("arbitrary", "arbitrary")),
        name="dn_prep",
    )(z, z, z, w_conv)


def _delta_kernel(seq_chunks_a, chunks_a, n_chunks, nh, hd, *refs):
    in_f, in_b = refs[0:5], refs[5:10]
    of_ref, ob_ref, sf_ref, sb_ref = refs[10:14]
    i = pl.program_id(0)
    cb = n_chunks - 1 - i

    @pl.when((i % seq_chunks_a == 0) & (i <= chunks_a))
    def _():
        sf_ref[...] = jnp.zeros_like(sf_ref)

    @pl.when((((cb + 1) % seq_chunks_a == 0) & (cb + 1 <= chunks_a)) | (cb + 1 == n_chunks))
    def _():
        sb_ref[...] = jnp.zeros_like(sb_ref)

    out_f, st_f = _delta_chunk(False, nh, hd, *in_f, sf_ref)
    out_b, st_b = _delta_chunk(True, nh, hd, *in_b, sb_ref)
    of_ref[...] = out_f
    ob_ref[...] = out_b
    for h in range(nh):
        sf_ref[h] = st_f[h]
        sb_ref[h] = st_b[h]


def _delta_chunk(reverse, nh, hd, q_ref, k_ref, v_ref, gcol_ref, grow_ref, s_ref):
    cs = q_ref.shape[0]
    r = lax.broadcasted_iota(I32, (cs, 2 * cs), 0)
    lane = lax.broadcasted_iota(I32, (cs, 2 * cs), 1)
    left = lane < cs
    cc = jnp.where(left, lane, lane - cs)
    ahead = (cc - r) if reverse else (r - cc)
    tri = ahead >= 0
    strict = ahead > 0
    eye = (r == cc).astype(F32)
    left_w = lax.broadcasted_iota(I32, (cs, 2 * hd), 1) < hd
    dot = functools.partial(jnp.dot, preferred_element_type=F32)

    def block_diag(x, is_left):
        zero = jnp.zeros_like(x)
        return jnp.concatenate([jnp.where(is_left, x, zero), jnp.where(is_left, zero, x)], axis=0)

    nt_dims = (((1,), (1,)), ((), ()))
    tn_dims = (((0,), (0,)), ((), ()))
    n_sq = max(1, (cs - 1).bit_length() - 1)
    pairs = list(range(0, nh, 2))
    heads = list(range(nh))
    cols = [slice(h * hd, (h + 1) * hd) for h in heads]
    kf = [k_ref[:, s_].astype(F32) for s_ in cols]
    gc = [gcol_ref[:, h:h + 1] for h in heads]
    beta = [gcol_ref[:, nh + h:nh + h + 1] for h in heads]
    gr = [gcol_ref[:, 2 * nh + h:2 * nh + h + 1] for h in heads]
    s_old = [s_ref[h] for h in heads]
    sb = [a.astype(BF16) for a in s_old]
    eg = [jnp.exp(a) for a in gc]
    qks = [dot(jnp.concatenate([(q_ref[:, s_].astype(F32) * e).astype(BF16), (c * (b * e)).astype(BF16)], axis=0), st)
           for s_, c, b, e, st in zip(cols, kf, beta, eg, sb)]
    cols2 = [slice(h * hd, (h + 2) * hd) for h in pairs]
    k2 = [k_ref[:, s_] for s_ in cols2]
    q2 = [q_ref[:, s_] for s_ in cols2]
    gc2 = [jnp.where(left, gc[h], gc[h + 1]) for h in pairs]
    beta2 = [jnp.where(left, beta[h], beta[h + 1]) for h in pairs]
    gc_row2 = [jnp.concatenate([grow_ref[h:h + 1, :], grow_ref[h + 1:h + 2, :]], axis=1) for h in pairs]
    decay = [jnp.exp(jnp.where(tri, a - b, -jnp.inf)) for a, b in zip(gc2, gc_row2)]
    kq = [lax.dot_general(jnp.concatenate([a, b], axis=0), block_diag(a, left_w), nt_dims,
                          preferred_element_type=F32) for a, b in zip(k2, q2)]
    low = [jnp.where(strict, b * a[:cs] * dcy, 0.0) for a, b, dcy in zip(kq, beta2, decay)]
    attn = [(a[cs:] * dcy).astype(BF16) for a, dcy in zip(kq, decay)]
    inv = [eye - a for a in low]
    x = [dot(a.astype(BF16), block_diag(a, left).astype(BF16)) for a in low]
    for j in range(n_sq):
        w = [block_diag(a, left).astype(BF16) for a in x]
        if j + 1 < n_sq:
            both = [dot(jnp.concatenate([p, a], axis=0).astype(BF16), b) for p, a, b in zip(inv, x, w)]
            inv = [p + a[:cs] for p, a in zip(inv, both)]
            x = [a[cs:] for a in both]
        else:
            inv = [p + dot(p.astype(BF16), b) for p, b in zip(inv, w)]
    rhs = [jnp.concatenate([v_ref[:, cols[h]].astype(F32) * beta[h] - qks[h][cs:],
                            v_ref[:, cols[h + 1]].astype(F32) * beta[h + 1] - qks[h + 1][cs:]], axis=1)
           for h in pairs]
    v_new = [dot(p.astype(BF16), block_diag(a, left_w).astype(BF16)) for p, a in zip(inv, rhs)]
    vb = [a.astype(BF16) for a in v_new]
    outs = [jnp.concatenate([qks[h][:cs], qks[h + 1][:cs]], axis=1) + dot(a, block_diag(b, left_w))
            for h, a, b in zip(pairs, attn, vb)]
    kd = [(a * jnp.exp(b)).astype(BF16) for a, b in zip(kf, gr)]
    g_tot = [jnp.exp(a[0:1, :] + b[0:1, :]) for a, b in zip(gc, gr)]
    states = [s_old[h] * g_tot[h]
              + lax.dot_general(kd[h], vb[h // 2][:, (h % 2) * hd:(h % 2 + 1) * hd], tn_dims,
                                preferred_element_type=F32) for h in heads]
    return jnp.concatenate(outs, axis=1), states


def delta_rule(qkv, gcol, grow, nh, hd, seq_a, n_a):
    t = qkv.shape[0]
    width = nh * hd
    cs = DN_CHUNK
    n_chunks = t // cs
    ng = gcol.shape[-1]
    assert nh % 2 == 0 and 2 * cs == LANES
    grow = grow.reshape(2, ng, n_chunks, cs).transpose(0, 2, 1, 3)

    def specs(d):
        chunk = (lambda i: i) if d == 0 else (lambda i: n_chunks - 1 - i)
        return [pl.BlockSpec((cs, width), lambda i: (chunk(i), 0)),
                pl.BlockSpec((cs, width), lambda i: (chunk(i), 1)),
                pl.BlockSpec((cs, width), lambda i: (chunk(i), 2)),
                pl.BlockSpec((None, cs, ng), lambda i: (d, chunk(i), 0)),
                pl.BlockSpec((None, None, ng, cs), lambda i: (d, chunk(i), 0, 0))]

    out = jax.ShapeDtypeStruct((t, width), F32)
    return pl.pallas_call(
        functools.partial(_delta_kernel, seq_a // cs, n_a // cs, n_chunks, nh, hd),
        grid=(n_chunks,),
        in_specs=specs(0) + specs(1),
        out_specs=[pl.BlockSpec((cs, width), lambda i: (i, 0)),
                   pl.BlockSpec((cs, width), lambda i: (n_chunks - 1 - i, 0))],
        out_shape=[out, out],
        scratch_shapes=[pltpu.VMEM((nh, hd, hd), F32), pltpu.VMEM((nh, hd, hd), F32)],
        compiler_params=_params(("arbitrary",)),
        name="delta_rule",
    )(qkv, qkv, qkv, gcol, grow, qkv, qkv, qkv, gcol, grow)


def _dn_post_kernel(hd, col_block, of_ref, ob_ref, og_ref, nw_ref, y_ref):
    del col_block
    o = of_ref[...] + ob_ref[...]
    og = og_ref[...].astype(F32)
    nw = nw_ref[...]
    for h in range(o.shape[1] // hd):
        hs = slice(h * hd, (h + 1) * hd)
        oh = o[:, hs]
        inv = lax.rsqrt(jnp.mean(oh * oh, -1, keepdims=True) + RMS_EPS)
        y_ref[:, hs] = (oh * inv * nw * _silu(og[:, hs])).astype(y_ref.dtype)


def dn_post(o_f, o_b, z, og_block, norm_w, hd, tm=256):
    t, width = o_f.shape
    return pl.pallas_call(
        functools.partial(_dn_post_kernel, hd, og_block),
        grid=(t // tm,),
        in_specs=[pl.BlockSpec((tm, width), lambda i: (i, 0)),
                  pl.BlockSpec((tm, width), lambda i: (i, 0)),
                  pl.BlockSpec((tm, width), lambda i: (i, og_block)),
                  pl.BlockSpec((1, hd), lambda i: (0, 0))],
        out_specs=pl.BlockSpec((tm, width), lambda i: (i, 0)),
        out_shape=jax.ShapeDtypeStruct((t, width), BF16),
        compiler_params=_params(("arbitrary",)),
        name="dn_post",
    )(o_f, o_b, z, norm_w.reshape(1, hd).astype(F32))


def _sc_kernel(seq_a, n_a, total, b_ref, c_ref, cprev_ref, cnext_ref, h_ref, hprev_ref, hnext_ref, w_ref, y_ref):
    i = pl.program_id(0)
    tm = b_ref.shape[0]
    starts, ends = _seq_edges(i * tm, tm, seq_a, n_a, total)
    x = c_ref[...].astype(F32) * h_ref[...].astype(F32)
    prev_row = jnp.where(starts, 0.0, cprev_ref[HALO - 1:HALO, :].astype(F32) * hprev_ref[HALO - 1:HALO, :].astype(F32))
    next_row = jnp.where(ends, 0.0, cnext_ref[0:1, :].astype(F32) * hnext_ref[0:1, :].astype(F32))
    xm, xp = _shifted(x, prev_row, next_row)
    conv = xm * w_ref[0:1, :] + x * w_ref[1:2, :] + xp * w_ref[2:3, :]
    y_ref[...] = (b_ref[...].astype(F32) * conv).astype(y_ref.dtype)


def short_conv(z, w_conv, first_block, width, seq_a, n_a, tm=256):
    t = z.shape[0]
    return pl.pallas_call(
        functools.partial(_sc_kernel, seq_a, n_a, t),
        grid=(t // tm,),
        in_specs=([pl.BlockSpec((tm, width), lambda i: (i, first_block))]
                  + _halo_specs(tm, width, lambda i: first_block + 1, t)
                  + _halo_specs(tm, width, lambda i: first_block + 2, t)
                  + [pl.BlockSpec((3, width), lambda i: (0, 0))]),
        out_specs=pl.BlockSpec((tm, width), lambda i: (i, 0)),
        out_shape=jax.ShapeDtypeStruct((t, width), BF16),
        compiler_params=_params(("arbitrary",)),
        name="short_conv",
    )(z, z, z, z, z, z, z, w_conv)


def _xattn_kernel(scale, q_ref, k_ref, v_ref, o_ref):
    s = lax.dot_general(q_ref[...], k_ref[...], (((1,), (1,)), ((), ())), preferred_element_type=F32) * scale
    m = jnp.max(s, -1, keepdims=True)
    e = jnp.exp(s - m)
    p = e / jnp.sum(e, -1, keepdims=True)
    o_ref[...] = jnp.dot(p.astype(BF16), v_ref[...], preferred_element_type=F32).astype(o_ref.dtype)


def mem_attention(z, kv, q_block0, n_heads, hd, n_mem, seq_a, n_a, tm=512):
    t = z.shape[0]
    n_seq_a = n_a // seq_a

    def seq_of(i):
        tok = i * tm
        return jnp.where(tok < n_a, tok // seq_a, n_seq_a)

    return pl.pallas_call(
        functools.partial(_xattn_kernel, hd ** -0.5),
        grid=(t // tm, n_heads),
        in_specs=[pl.BlockSpec((tm, hd), lambda i, h: (i, q_block0 + h)),
                  pl.BlockSpec((n_mem, hd), lambda i, h: (seq_of(i), h)),
                  pl.BlockSpec((n_mem, hd), lambda i, h: (seq_of(i), n_heads + h))],
        out_specs=pl.BlockSpec((tm, hd), lambda i, h: (i, h)),
        out_shape=jax.ShapeDtypeStruct((t, n_heads * hd), BF16),
        compiler_params=_params(("arbitrary", "arbitrary")),
        name="mem_attention",
    )(z, kv, kv)


def _merge_kernel(nb, *refs):
    y_refs = refs[:nb]
    g_refs = refs[nb:2 * nb]
    w_ref = refs[2 * nb]
    o_ref = refs[2 * nb + 1]
    acc = None
    for b in range(nb):
        term = _sigmoid(g_refs[b][...].astype(F32)) * jnp.dot(y_refs[b][...], w_ref[b], preferred_element_type=F32)
        acc = term if acc is None else acc + term
    o_ref[...] = acc.astype(o_ref.dtype)


def gated_merge(ys, z, gate_col0, w_branch, tm=1024, tn=512):
    nb = len(ys)
    t, width = ys[0].shape
    d = w_branch.shape[-1]
    g0 = gate_col0 // tn
    per = d // tn
    gate_spec = lambda b: pl.BlockSpec((tm, tn), lambda i, j: (i, g0 + b * per + j))
    return pl.pallas_call(
        functools.partial(_merge_kernel, nb),
        grid=(t // tm, d // tn),
        in_specs=([pl.BlockSpec((tm, width), lambda i, j: (i, 0))] * nb
                  + [gate_spec(b) for b in range(nb)]
                  + [pl.BlockSpec((nb, width, tn), lambda i, j: (0, 0, j))]),
        out_specs=pl.BlockSpec((tm, tn), lambda i, j: (i, j)),
        out_shape=jax.ShapeDtypeStruct((t, d), BF16),
        compiler_params=_params(("arbitrary", "arbitrary")),
        name="gated_merge",
    )(*ys, *([z] * nb), w_branch)


def _res_ln1_kernel(x_ref, y_ref, g_ref, b_ref, of_ref, op_ref):
    h = _layer_norm(DEEPNORM_ALPHA * x_ref[...] + y_ref[...], g_ref[...], b_ref[...])
    of_ref[...] = h
    half = h.shape[1] // 2
    op_ref[...] = _pack_bf16_pair(h[:, :half], h[:, half:])


def res_ln1(x, y, g, b, tm=256):
    t, d = x.shape
    row = lambda w: pl.BlockSpec((tm, w), lambda i: (i, 0))
    vec = pl.BlockSpec((1, d), lambda i: (0, 0))
    return pl.pallas_call(
        _res_ln1_kernel,
        grid=(t // tm,),
        in_specs=[row(d), row(d), vec, vec],
        out_specs=[row(d), row(d // 2)],
        out_shape=[jax.ShapeDtypeStruct((t, d), F32), jax.ShapeDtypeStruct((t, d // 2), U32)],
        compiler_params=_params(("arbitrary",)),
        name="res_ln1",
    )(x, y, g.reshape(1, d), b.reshape(1, d))


def _first_index_of_max(x, idx, big):
    m = jnp.max(x, axis=0, keepdims=True)
    return m, jnp.min(jnp.where(x == m, idx, big), axis=0, keepdims=True)


def _router_kernel(ne, ng, kg, topk, h_ref, wt_ref, bias_ref, eidx_ref, gate_ref, rank_ref, cnt_ref, run_ref):
    i = pl.program_id(0)
    tm = h_ref.shape[0]
    per = ne // ng

    @pl.when(i == 0)
    def _():
        run_ref[...] = jnp.zeros_like(run_ref)

    logits = lax.dot_general(wt_ref[...], h_ref[...], (((1,), (1,)), ((), ())),
                             preferred_element_type=F32, precision=lax.Precision.HIGHEST)
    scores = _sigmoid(logits)
    sel = scores + bias_ref[...]
    eid = lax.broadcasted_iota(I32, (ne, tm), 0)
    neg = -jnp.inf
    pid = lax.broadcasted_iota(I32, (per, tm), 0)
    gscores = []
    for g in range(ng):
        sg = sel[g * per:(g + 1) * per, :]
        m1, a1 = _first_index_of_max(sg, pid, per)
        m2 = jnp.max(jnp.where(pid == a1, neg, sg), axis=0, keepdims=True)
        gscores.append(m1 + m2)
    gscore = jnp.concatenate(gscores, axis=0)
    gid = lax.broadcasted_iota(I32, (ng, tm), 0)
    egroup = eid // per
    keep_e = jnp.zeros((ne, tm), jnp.bool_)
    for _ in range(kg):
        _, a = _first_index_of_max(gscore, gid, ng)
        keep_e = keep_e | (egroup == a)
        gscore = jnp.where(gid == a, neg, gscore)
    cand = jnp.where(keep_e, sel, neg)
    onehot = jnp.zeros((ne, tm), F32)
    idxs, gates = [], []
    for _ in range(topk):
        _, a = _first_index_of_max(cand, eid, ne)
        hit = eid == a
        idxs.append(a)
        gates.append(jnp.sum(jnp.where(hit, scores, 0.0), axis=0, keepdims=True))
        onehot = jnp.where(hit, 1.0, onehot)
        cand = jnp.where(hit, neg, cand)
    gate = jnp.concatenate(gates, axis=0)
    gate = gate / jnp.sum(gate, axis=0, keepdims=True) * ROUTE_SCALE
    s_ = lax.broadcasted_iota(I32, (tm, tm), 0)
    t_ = lax.broadcasted_iota(I32, (tm, tm), 1)
    before = jnp.where(s_ < t_, 1.0, 0.0).astype(BF16)
    prior = jnp.dot(onehot.astype(BF16), before, preferred_element_type=F32) + run_ref[...]
    ranks = [jnp.sum(jnp.where(eid == a, prior, 0.0), axis=0, keepdims=True) for a in idxs]
    eidx_ref[...] = jnp.concatenate(idxs, axis=0)
    gate_ref[...] = gate
    rank_ref[...] = jnp.concatenate(ranks, axis=0).astype(I32)
    run_ref[...] = run_ref[...] + jnp.sum(onehot, axis=1, keepdims=True)
    cnt_ref[...] = run_ref[...].astype(I32)


def router(h_f32, w_router, bias, tm=256):
    t, d = h_f32.shape
    ne = w_router.shape[1]
    out = lambda dt: jax.ShapeDtypeStruct((TOP_K, t), dt)
    tok = pl.BlockSpec((TOP_K, tm), lambda i: (0, i))
    return pl.pallas_call(
        functools.partial(_router_kernel, ne, N_EXPERT_GROUPS, TOPK_GROUPS, TOP_K),
        grid=(t // tm,),
        in_specs=[pl.BlockSpec((tm, d), lambda i: (i, 0)), pl.BlockSpec((ne, d), lambda i: (0, 0)),
                  pl.BlockSpec((ne, 1), lambda i: (0, 0))],
        out_specs=[tok, tok, tok, pl.BlockSpec((ne, 1), lambda i: (0, 0))],
        out_shape=[out(I32), out(F32), out(I32), jax.ShapeDtypeStruct((ne, 1), I32)],
        scratch_shapes=[pltpu.VMEM((ne, 1), F32)],
        compiler_params=_params(("arbitrary",)),
        name="router",
    )(h_f32, w_router.T.astype(F32), bias.reshape(ne, 1).astype(F32))


def _zero_blocks_kernel(blk_ref, o_ref):
    del blk_ref
    o_ref[...] = jnp.zeros_like(o_ref)


def zero_blocks(block_ids, n_rows, w, tm):
    return pl.pallas_call(
        _zero_blocks_kernel,
        grid_spec=pltpu.PrefetchScalarGridSpec(
            num_scalar_prefetch=1,
            grid=(block_ids.shape[0],),
            in_specs=[],
            out_specs=pl.BlockSpec((tm, w), lambda e, blk: (blk[e], 0)),
        ),
        out_shape=jax.ShapeDtypeStruct((n_rows, w), U32),
        compiler_params=_params(("arbitrary",)),
        name="moe_pad_zero",
    )(block_ids)


def _dispatch_kernel(topk, pos_ref, h_ref, dst_in_ref, dst_ref, sem):
    del dst_in_ref
    tb = h_ref.shape[0]

    def issue(r, carry):
        for k in range(topk):
            pltpu.make_async_copy(h_ref.at[pl.ds(r, 1), :], dst_ref.at[pl.ds(pos_ref[0, k, r], 1), :], sem).start()
        return carry

    lax.fori_loop(0, tb, issue, 0, unroll=4)
    for k in range(topk):
        pltpu.make_async_copy(h_ref, dst_ref.at[pl.ds(0, tb), :], sem).wait()


def moe_dispatch(h_pk, pos3, dst, tb):
    t, w = h_pk.shape
    nb, topk, _ = pos3.shape
    return pl.pallas_call(
        functools.partial(_dispatch_kernel, topk),
        grid=(nb,),
        in_specs=[pl.BlockSpec((1, topk, tb), lambda i: (i, 0, 0), memory_space=pltpu.SMEM),
                  pl.BlockSpec((tb, w), lambda i: (i, 0)),
                  pl.BlockSpec(memory_space=pl.ANY)],
        out_specs=pl.BlockSpec(memory_space=pl.ANY),
        out_shape=jax.ShapeDtypeStruct(dst.shape, dst.dtype),
        input_output_aliases={2: 0},
        scratch_shapes=[pltpu.SemaphoreType.DMA(())],
        compiler_params=_params(("arbitrary",)),
        name="moe_dispatch",
    )(pos3, h_pk, dst)


GMM_K_CHUNKS = 4


def _gmm_up_kernel(be_ref, nb_ref, x_ref, wg_ref, wu_ref, o_ref):
    b = pl.program_id(1)

    @pl.when(b < nb_ref[0])
    def _():
        words = x_ref.shape[1]
        kc = 2 * words // GMM_K_CHUNKS
        per_half = GMM_K_CHUNKS // 2
        g = u = None
        for c in range(GMM_K_CHUNKS):
            w = x_ref[:, (c % per_half) * kc:(c % per_half + 1) * kc]
            bits = (w << 16) if c < per_half else (w & jnp.uint32(0xFFFF0000))
            xc = lax.bitcast_convert_type(bits, F32).astype(BF16)
            rows = slice(c * kc, (c + 1) * kc)
            gc = jnp.dot(xc, wg_ref[rows, :].astype(BF16), preferred_element_type=F32)
            uc = jnp.dot(xc, wu_ref[rows, :].astype(BF16), preferred_element_type=F32)
            g = gc if g is None else g + gc
            u = uc if u is None else u + uc
        o_ref[...] = (_silu(g) * u).astype(o_ref.dtype)

    @pl.when(b >= nb_ref[0])
    def _():
        o_ref[...] = jnp.zeros_like(o_ref)


def gmm_up(x, w_gate, w_up, block_expert, n_used, tm, name, tn=512):
    rows, xw = x.shape
    _, d, f = w_gate.shape
    tn = _tile(f, tn)
    assert d == 2 * xw and (d // GMM_K_CHUNKS) % LANES == 0
    wspec = pl.BlockSpec((None, d, tn), lambda j, b, be, nb: (be[b], 0, j))
    return pl.pallas_call(
        _gmm_up_kernel,
        grid_spec=pltpu.PrefetchScalarGridSpec(
            num_scalar_prefetch=2,
            grid=(f // tn, rows // tm),
            in_specs=[pl.BlockSpec((tm, xw), lambda j, b, be, nb: (jnp.minimum(b, nb[0] - 1), 0)), wspec, wspec],
            out_specs=pl.BlockSpec((tm, tn), lambda j, b, be, nb: (b, j)),
        ),
        out_shape=jax.ShapeDtypeStruct((rows, f), BF16),
        compiler_params=_params(("arbitrary", "arbitrary")),
        name=name,
    )(block_expert, n_used, x, w_gate, w_up)


def _gmm_down_kernel(n_ranges, be_ref, nb_ref, h_ref, w_ref, o_ref):
    b = pl.program_id(1)

    @pl.when(b < nb_ref[0])
    def _():
        h = h_ref[...]
        cw = w_ref.shape[1] // n_ranges
        for c in range(n_ranges):
            y = jnp.dot(h, w_ref[:, c * cw:(c + 1) * cw].astype(BF16), preferred_element_type=F32)
            o_ref[:, c * (cw // 2):(c + 1) * (cw // 2)] = _pack_bf16_pair(y[:, :cw // 2], y[:, cw // 2:])

    @pl.when(b >= nb_ref[0])
    def _():
        o_ref[...] = jnp.zeros_like(o_ref)


def down_ranges(d):
    return max(r for r in (4, 2, 1) if (d // r) % (2 * LANES) == 0)


def gmm_down(h, w_down, block_expert, n_used, tm, name):
    rows, f = h.shape
    _, _, d = w_down.shape
    return pl.pallas_call(
        functools.partial(_gmm_down_kernel, down_ranges(d)),
        grid_spec=pltpu.PrefetchScalarGridSpec(
            num_scalar_prefetch=2,
            grid=(1, rows // tm),
            in_specs=[pl.BlockSpec((tm, f), lambda j, b, be, nb: (jnp.minimum(b, nb[0] - 1), 0)),
                      pl.BlockSpec((None, f, d), lambda j, b, be, nb: (be[b], 0, 0))],
            out_specs=pl.BlockSpec((tm, d // 2), lambda j, b, be, nb: (b, 0)),
        ),
        out_shape=jax.ShapeDtypeStruct((rows, d // 2), U32),
        compiler_params=_params(("arbitrary", "arbitrary")),
        name=name,
    )(block_expert, n_used, h, w_down)


def _unpack_rows(w, n_split):
    q = w.shape[1] // n_split
    parts = []
    for s in range(n_split):
        lo, hi = _unpack_bf16_pair(w[:, s * q:(s + 1) * q])
        parts += [lo, hi]
    return jnp.concatenate(parts, axis=1)


COMBINE_DEPTH = 3


def _combine_kernel(topk, n_split, half_blocks, pos_ref, pos1_ref, pos2_ref, y_ref, h_ref, sh_ref, gate_ref, g_ref,
                    b_ref, op_ref, os_ref, *scratch):
    bufs, sem = scratch[:COMBINE_DEPTH], scratch[COMBINE_DEPTH]
    i = pl.program_id(0)
    n = pl.num_programs(0)
    tb = h_ref.shape[0]

    def issue(p_ref, s):
        def body(r, carry):
            for k in range(topk):
                pltpu.make_async_copy(y_ref.at[pl.ds(p_ref[0, k, r], 1), :], bufs[s].at[k, pl.ds(r, 1), :],
                                      sem.at[s]).start()
            return carry
        lax.fori_loop(0, tb, body, 0, unroll=True)

    def drain(s):
        for k in range(topk):
            pltpu.make_async_copy(y_ref.at[pl.ds(0, tb), :], bufs[s].at[k], sem.at[s]).wait()

    @pl.when(i == 0)
    def _():
        issue(pos_ref, 0)
        issue(pos1_ref, 1)

    def step(s):
        drain(s)
        issue(pos2_ref, (s + 2) % COMBINE_DEPTH)
        acc = DEEPNORM_ALPHA * h_ref[...] + _unpack_rows(sh_ref[...], n_split)
        for k in range(topk):
            acc = acc + gate_ref[:, k:k + 1] * _unpack_rows(bufs[s][k], n_split)
        out = _layer_norm(acc, g_ref[...], b_ref[...])

        @pl.when(i < half_blocks)
        def _():
            op_ref[...] = out

        @pl.when(i >= half_blocks)
        def _():
            os_ref[...] = out

        @pl.when(i == n - 1)
        def _():
            drain((s + 1) % COMBINE_DEPTH)
            drain((s + 2) % COMBINE_DEPTH)

    for s in range(COMBINE_DEPTH):
        pl.when(i % COMBINE_DEPTH == s)(functools.partial(step, s))


def moe_combine(y_sorted, pos3, gate, h_f32, shared, g, b, n_first, n_split):
    t, d = h_f32.shape
    nb, topk, tb = pos3.shape
    hb = n_first // tb
    smem = lambda f: pl.BlockSpec((1, topk, tb), f, memory_space=pltpu.SMEM)
    row = lambda w: pl.BlockSpec((tb, w), lambda i: (i, 0))
    vec = pl.BlockSpec((1, d), lambda i: (0, 0))
    return pl.pallas_call(
        functools.partial(_combine_kernel, topk, n_split, hb),
        grid=(nb,),
        in_specs=[smem(lambda i: (i, 0, 0)), smem(lambda i: (jnp.minimum(i + 1, nb - 1), 0, 0)),
                  smem(lambda i: (jnp.minimum(i + 2, nb - 1), 0, 0)),
                  pl.BlockSpec(memory_space=pl.ANY), row(d), row(d // 2), row(topk), vec, vec],
        out_specs=[pl.BlockSpec((tb, d), lambda i: (jnp.minimum(i, hb - 1), 0)),
                   pl.BlockSpec((tb, d), lambda i: (jnp.maximum(i - hb, 0), 0))],
        out_shape=[jax.ShapeDtypeStruct((n_first, d), F32), jax.ShapeDtypeStruct((t - n_first, d), F32)],
        scratch_shapes=([pltpu.VMEM((topk, tb, d // 2), U32)] * COMBINE_DEPTH
                        + [pltpu.SemaphoreType.DMA((COMBINE_DEPTH,))]),
        compiler_params=_params(("arbitrary",)),
        name="moe_combine",
    )(pos3, pos3, pos3, y_sorted, h_f32, shared, gate, g.reshape(1, d), b.reshape(1, d))


MOE_TM = 512
MOE_TB = 64


def _moe_schedule(eidx, rank, counts, tm, tb):
    topk, t = eidx.shape
    ne = counts.shape[0]
    n_blocks = (t * topk + ne * (tm - 1) + tm - 1) // tm
    padded = (counts + tm - 1) // tm * tm
    ends = jnp.cumsum(padded)
    starts = ends - padded
    start_of = jnp.sum(jnp.where(eidx[:, :, None] == jnp.arange(ne, dtype=I32), starts.astype(I32), 0), -1)
    pos = (start_of + rank).astype(I32)
    pos3 = pos.reshape(topk, t // tb, tb).transpose(1, 0, 2)
    n_used = (ends[-1] // tm).astype(I32)
    blk = jnp.arange(n_blocks, dtype=I32)
    blk_e = jnp.searchsorted(ends, jnp.minimum(blk, n_used - 1) * tm, side="right").astype(I32)
    blk_e = jnp.minimum(blk_e, ne - 1)
    pad_blk = jnp.where(counts > 0, ends // tm - 1, jnp.minimum(starts // tm, n_blocks - 1)).astype(I32)
    return pos3, blk_e, n_used.reshape(1), pad_blk, n_blocks * tm


def kernel(x_prompt, x_sample, mem_prompt, mem_sample, ln_in_g, ln_in_b, w_in, w_dn_conv, dn_a_log, dn_dt_bias,
           dn_norm_w, w_sc_conv, w_mem_kv, w_branch, w_o, ln1_g, ln1_b, w_router, router_bias, w_gate_e, w_up_e,
           w_down_e, w_gate_s, w_up_s, w_down_s, ln2_g, ln2_b):
    assert w_in.shape[0] == DEPTH == 1
    bp, sp, d = x_prompt.shape
    bs, ss, _ = x_sample.shape
    assert bs == 1
    n_a, seq_a = bp * sp, sp
    t = n_a + bs * ss
    nh = dn_a_log.shape[-1]
    hd = dn_norm_w.shape[-1]
    dn_w = nh * hd
    sc_w = w_sc_conv.shape[-1]
    x_w = w_mem_kv.shape[-1] // 2
    n_mem = mem_prompt.shape[1]
    nbr = w_branch.shape[1]
    assert dn_w == sc_w == x_w, "column blocks of the combined projection are addressed in units of one width"
    blk = dn_w
    lyr = 0

    w_all = w_in[lyr]
    ab0 = 4 * dn_w
    w_main = drop_columns(w_in, lyr, ab0, 4 * nh)
    w_ab = w_all[:, ab0:ab0 + 4 * nh].astype(BF16)

    h0_f, h0_b = ln_in(x_prompt.reshape(n_a, d), x_sample.reshape(t - n_a, d), ln_in_g, ln_in_b)
    z = matmul(h0_b, w_main, 1024, 1024, BF16, "in_proj")

    gcol, grow = dn_gates(h0_b, w_ab, dn_a_log[lyr], dn_dt_bias[lyr], nh, DN_CHUNK)
    qkv = dn_prep(z, w_dn_conv[lyr].astype(F32), dn_w, hd, seq_a, n_a)
    o_f, o_b = delta_rule(qkv, gcol, grow, nh, hd, seq_a, n_a)
    y_dn = dn_post(o_f, o_b, z, 3, dn_norm_w[lyr], hd)
    y_sc = short_conv(z, w_sc_conv[lyr].astype(F32), 4, sc_w, seq_a, n_a)
    mem = jnp.concatenate([mem_prompt.reshape(bp * n_mem, d), mem_sample.reshape(bs * n_mem, d)], 0).astype(BF16)
    kv = matmul(mem, w_mem_kv[lyr].astype(BF16), 256, 1024, BF16, "mem_kv")
    xhd = x_w // N_X_HEADS
    y_mem = mem_attention(z, kv, 7 * blk // xhd, N_X_HEADS, xhd, n_mem, seq_a, n_a)

    mixed = gated_merge([y_dn, y_sc, y_mem], z, 8 * blk, w_branch[lyr].astype(BF16))
    attn_out = matmul(mixed, w_o[lyr].astype(BF16), 1024, 1024, F32, "out_proj")
    h_f, h_pk = res_ln1(h0_f, attn_out, ln1_g[lyr], ln1_b[lyr])

    eidx, gate, rank, counts = router(h_f, w_router[lyr], router_bias[lyr])
    pos3, blk_e, n_used, pad_blk, n_rows = _moe_schedule(eidx, rank, counts.reshape(-1), MOE_TM, MOE_TB)
    x_sorted = moe_dispatch(h_pk, pos3, zero_blocks(pad_blk, n_rows, d // 2, MOE_TM), MOE_TB)
    hid = gmm_up(x_sorted, w_gate_e[lyr], w_up_e[lyr], blk_e, n_used, MOE_TM, "moe_up")
    y_sorted = gmm_down(hid, w_down_e[lyr], blk_e, n_used, MOE_TM, "moe_down")
    one = jnp.zeros((t // MOE_TM,), I32)
    all_blocks = jnp.full((1,), t // MOE_TM, I32)
    hid_s = gmm_up(h_pk, w_gate_s, w_up_s, one, all_blocks, MOE_TM, "shared_up")
    y_shared = gmm_down(hid_s, w_down_s, one, all_blocks, MOE_TM, "shared_down")
    y_p, y_s = moe_combine(y_sorted, pos3, gate.T, h_f, y_shared, ln2_g[lyr], ln2_b[lyr], n_a, down_ranges(d))
    return y_p.reshape(bp, sp, d), y_s.reshape(bs, ss, d)
```

```python
import functools

import jax
import jax.numpy as jnp
from jax import lax
from jax.experimental import pallas as pl
from jax.experimental.pallas import tpu as pltpu

F32 = jnp.float32
BF16 = jnp.bfloat16
I32 = jnp.int32
U32 = jnp.uint32

DN_CHUNK = 64
N_X_HEADS = 4
N_EXPERT_GROUPS = 8
TOPK_GROUPS = 4
TOP_K = 8
ROUTE_SCALE = 2.5
DEPTH = 1
DEEPNORM_ALPHA = (2 * DEPTH) ** 0.25
LN_EPS = 1e-5
RMS_EPS = 1e-6
L2_EPS = 1e-6

V7X_VMEM_BYTES = 64 * 1024 * 1024
VMEM_LIMIT = 56 * 1024 * 1024
LANES = 128
HALO = 16


def _params(sem):
    return pltpu.CompilerParams(dimension_semantics=sem, vmem_limit_bytes=VMEM_LIMIT)


def _sigmoid(x):
    return 1.0 / (1.0 + jnp.exp(-x))


def _silu(x):
    return x * _sigmoid(x)


def _softplus(x):
    return jnp.maximum(x, 0.0) + jnp.log1p(jnp.exp(-jnp.abs(x)))


def _tile(n, pref):
    while n % pref:
        pref //= 2
    return pref


def _layer_norm(x, g, b):
    mu = jnp.mean(x, -1, keepdims=True)
    xc = x - mu
    var = jnp.mean(xc * xc, -1, keepdims=True)
    return xc * lax.rsqrt(var + LN_EPS) * g + b


def _pack_bf16_pair(lo, hi):
    lo_b = lax.bitcast_convert_type(lo.astype(BF16).astype(F32), U32)
    hi_b = lax.bitcast_convert_type(hi.astype(BF16).astype(F32), U32)
    return (hi_b & jnp.uint32(0xFFFF0000)) | (lo_b >> 16)


def _unpack_bf16_pair(w):
    lo = lax.bitcast_convert_type(w << 16, F32)
    hi = lax.bitcast_convert_type(w & jnp.uint32(0xFFFF0000), F32)
    return lo, hi


def _ln_in_kernel(half_blocks, xp_ref, xs_ref, g_ref, b_ref, of_ref, ob_ref):
    i = pl.program_id(0)

    def emit(x_ref):
        y = _layer_norm(x_ref[...], g_ref[...], b_ref[...])
        of_ref[...] = y
        ob_ref[...] = y.astype(BF16)

    @pl.when(i < half_blocks)
    def _():
        emit(xp_ref)

    @pl.when(i >= half_blocks)
    def _():
        emit(xs_ref)


def ln_in(xp, xs, g, b, tm=256):
    tp, d = xp.shape
    ts = xs.shape[0]
    hb = tp // tm
    nb = hb + ts // tm
    return pl.pallas_call(
        functools.partial(_ln_in_kernel, hb),
        grid=(nb,),
        in_specs=[
            pl.BlockSpec((tm, d), lambda i: (jnp.minimum(i, hb - 1), 0)),
            pl.BlockSpec((tm, d), lambda i: (jnp.maximum(i - hb, 0), 0)),
            pl.BlockSpec((1, d), lambda i: (0, 0)),
            pl.BlockSpec((1, d), lambda i: (0, 0)),
        ],
        out_specs=[pl.BlockSpec((tm, d), lambda i: (i, 0)), pl.BlockSpec((tm, d), lambda i: (i, 0))],
        out_shape=[jax.ShapeDtypeStruct((tp + ts, d), F32), jax.ShapeDtypeStruct((tp + ts, d), BF16)],
        compiler_params=_params(("arbitrary",)),
        name="ln_in",
    )(xp, xs, g.reshape(1, d), b.reshape(1, d))


def _mm_kernel(a_ref, b_ref, o_ref):
    o_ref[...] = jnp.dot(a_ref[...], b_ref[...], preferred_element_type=F32).astype(o_ref.dtype)


def matmul(a, b, tm, tn, out_dtype, name):
    m, k = a.shape
    n = b.shape[1]
    tm, tn = _tile(m, tm), _tile(n, tn)
    return pl.pallas_call(
        _mm_kernel,
        grid=(m // tm, n // tn),
        in_specs=[pl.BlockSpec((tm, k), lambda i, j: (i, 0)), pl.BlockSpec((k, tn), lambda i, j: (0, j))],
        out_specs=pl.BlockSpec((tm, tn), lambda i, j: (i, j)),
        out_shape=jax.ShapeDtypeStruct((m, n), out_dtype),
        compiler_params=_params(("arbitrary", "arbitrary")),
        name=name,
    )(a, b)


def _drop_columns_kernel(n_aligned, gap, a_ref, b_ref, o_ref, cut_ref):
    j = pl.program_id(1)

    @pl.when(j < n_aligned)
    def _():
        o_ref[...] = a_ref[...].astype(o_ref.dtype)

    @pl.when(j >= n_aligned)
    def _():
        o_ref[...] = jnp.concatenate([a_ref[:, gap:], b_ref[:, :gap]], axis=1).astype(o_ref.dtype)

    @pl.when(j == n_aligned)
    def _():
        cut_ref[...] = a_ref[:, :LANES].astype(cut_ref.dtype)


def drop_columns(w, layer, col0, gap, tr=512, tile=1024):
    _, rows, n_in = w.shape
    n_out = n_in - gap
    tile = _tile(n_out, tile)
    tr = _tile(rows, tr)
    assert col0 % tile == 0 and gap < LANES
    per = tile // LANES
    return pl.pallas_call(
        functools.partial(_drop_columns_kernel, col0 // tile, gap),
        grid=(rows // tr, n_out // tile),
        in_specs=[pl.BlockSpec((None, tr, tile), lambda i, j: (layer, i, j)),
                  pl.BlockSpec((None, tr, LANES), lambda i, j: (layer, i, (j + 1) * per))],
        out_specs=[pl.BlockSpec((tr, tile), lambda i, j: (i, j)), pl.BlockSpec((tr, LANES), lambda i, j: (i, 0))],
        out_shape=[jax.ShapeDtypeStruct((rows, n_out), BF16), jax.ShapeDtypeStruct((rows, LANES), BF16)],
        compiler_params=_params(("arbitrary", "arbitrary")),
        name="in_proj_weights",
    )(w, w)


def _dn_gates_kernel(chunk, nh, h_ref, w_ref, wt_ref, alog_ref, dtb_ref, alogt_ref, dtbt_ref, col_ref, row_ref):
    tm = h_ref.shape[0]
    h = h_ref[...]
    zc = jnp.dot(h, w_ref[...], preferred_element_type=F32)
    zr = lax.dot_general(wt_ref[...], h, (((1,), (1,)), ((), ())), preferred_element_type=F32)
    r = lax.broadcasted_iota(I32, (tm, tm), 0)
    c = lax.broadcasted_iota(I32, (tm, tm), 1)
    same = (r // chunk) == (c // chunk)
    le = jnp.where(same & (c <= r), 1.0, 0.0).astype(F32)
    ge = jnp.where(same & (c >= r), 1.0, 0.0).astype(F32)
    hp = lax.Precision.HIGHEST
    for d in range(2):
        a_c = zc[:, d * nh:(d + 1) * nh]
        b_c = zc[:, (2 + d) * nh:(3 + d) * nh]
        g_c = -jnp.exp(alog_ref[:, d * nh:(d + 1) * nh]) * _softplus(a_c + dtb_ref[:, d * nh:(d + 1) * nh])
        a_r = zr[d * nh:(d + 1) * nh, :]
        b_r = zr[(2 + d) * nh:(3 + d) * nh, :]
        g_r = -jnp.exp(alogt_ref[d * nh:(d + 1) * nh, :]) * _softplus(a_r + dtbt_ref[d * nh:(d + 1) * nh, :])
        incl, rest = (le, ge) if d == 0 else (ge, le)
        gc_c = jnp.dot(incl, g_c, preferred_element_type=F32, precision=hp)
        gr_c = jnp.dot(rest, g_c, preferred_element_type=F32, precision=hp) - g_c
        gc_r = jnp.dot(g_r, rest, preferred_element_type=F32, precision=hp)
        gr_r = jnp.dot(g_r, incl, preferred_element_type=F32, precision=hp) - g_r
        col_ref[d] = jnp.concatenate([gc_c, _sigmoid(b_c), gr_c], axis=1)
        row_ref[d] = jnp.concatenate([gc_r, _sigmoid(b_r), gr_r], axis=0)


def dn_gates(h_bf, w_ab, a_log, dt_bias, nh, chunk, tm=256):
    t, d = h_bf.shape
    w_t = w_ab.T
    alog = a_log.reshape(1, 2 * nh).astype(F32)
    dtb = dt_bias.reshape(1, 2 * nh).astype(F32)
    full = lambda shape: pl.BlockSpec(shape, lambda i: (0,) * len(shape))
    return pl.pallas_call(
        functools.partial(_dn_gates_kernel, chunk, nh),
        grid=(t // tm,),
        in_specs=[pl.BlockSpec((tm, d), lambda i: (i, 0)), full((d, 4 * nh)), full((4 * nh, d)),
                  full((1, 2 * nh)), full((1, 2 * nh)), full((2 * nh, 1)), full((2 * nh, 1))],
        out_specs=[pl.BlockSpec((2, tm, 3 * nh), lambda i: (0, i, 0)),
                   pl.BlockSpec((2, 3 * nh, tm), lambda i: (0, 0, i))],
        out_shape=[jax.ShapeDtypeStruct((2, t, 3 * nh), F32), jax.ShapeDtypeStruct((2, 3 * nh, t), F32)],
        compiler_params=_params(("arbitrary",)),
        name="dn_gates",
    )(h_bf, w_ab, w_t, alog, dtb, alog.T, dtb.T)


def _seq_edges(tok0, tm, seq_a, n_a, total):
    end = tok0 + tm
    starts = ((tok0 % seq_a == 0) & (tok0 <= n_a))
    ends = ((end % seq_a == 0) & (end <= n_a)) | (end == total)
    return starts, ends


def _shifted(x, prev_row, next_row):
    tm = x.shape[0]
    rows = lax.broadcasted_iota(I32, x.shape, 0)
    xm = jnp.where(rows == 0, prev_row, pltpu.roll(x, 1, 0))
    xp = jnp.where(rows == tm - 1, next_row, pltpu.roll(x, tm - 1, 0))
    return xm, xp


def _halo_specs(tm, width, col_block, n_rows):
    per = tm // HALO
    last = n_rows // HALO - 1
    return [
        pl.BlockSpec((tm, width), lambda i, *_: (i, col_block(i, *_))),
        pl.BlockSpec((HALO, width), lambda i, *_: (jnp.maximum(i * per - 1, 0), col_block(i, *_))),
        pl.BlockSpec((HALO, width), lambda i, *_: (jnp.minimum((i + 1) * per, last), col_block(i, *_))),
    ]


def _dn_prep_kernel(seq_a, n_a, total, hd, x_ref, xprev_ref, xnext_ref, w_ref, o_ref):
    i = pl.program_id(0)
    j = pl.program_id(1)
    tm, width = x_ref.shape
    starts, ends = _seq_edges(i * tm, tm, seq_a, n_a, total)
    x = x_ref[...].astype(F32)
    prev_row = jnp.where(starts, 0.0, xprev_ref[HALO - 1:HALO, :].astype(F32))
    next_row = jnp.where(ends, 0.0, xnext_ref[0:1, :].astype(F32))
    xm, xp = _shifted(x, prev_row, next_row)
    y = _silu(xm * w_ref[0:1, :] + x * w_ref[1:2, :] + xp * w_ref[2:3, :])

    @pl.when(j == 2)
    def _():
        o_ref[...] = y.astype(o_ref.dtype)

    @pl.when(j < 2)
    def _():
        scale = jnp.where(j == 0, hd ** -0.5, 1.0).astype(F32)
        for h in range(width // hd):
            yh = y[:, h * hd:(h + 1) * hd]
            inv = lax.rsqrt(jnp.sum(yh * yh, -1, keepdims=True) + L2_EPS) * scale
            o_ref[:, h * hd:(h + 1) * hd] = (yh * inv).astype(o_ref.dtype)


def dn_prep(z, w_conv, width, hd, seq_a, n_a, tm=256):
    t = z.shape[0]
    return pl.pallas_call(
        functools.partial(_dn_prep_kernel, seq_a, n_a, t, hd),
        grid=(t // tm, 3),
        in_specs=_halo_specs(tm, width, lambda i, j: j, t) + [pl.BlockSpec((3, width), lambda i, j: (0, j))],
        out_specs=pl.BlockSpec((tm, width), lambda i, j: (i, j)),
        out_shape=jax.ShapeDtypeStruct((t, 3 * width), BF16),
        compiler_params=_params(("arbitrary", "arbitrary")),
        name="dn_prep",
    )(z, z, z, w_conv)


def _delta_kernel(seq_chunks_a, chunks_a, n_chunks, nh, hd, *refs):
    in_f, in_b = refs[0:5], refs[5:10]
    of_ref, ob_ref, sf_ref, sb_ref = refs[10:14]
    i = pl.program_id(0)
    cb = n_chunks - 1 - i

    @pl.when((i % seq_chunks_a == 0) & (i <= chunks_a))
    def _():
        sf_ref[...] = jnp.zeros_like(sf_ref)

    @pl.when((((cb + 1) % seq_chunks_a == 0) & (cb + 1 <= chunks_a)) | (cb + 1 == n_chunks))
    def _():
        sb_ref[...] = jnp.zeros_like(sb_ref)

    out_f, st_f = _delta_chunk(False, nh, hd, *in_f, sf_ref)
    out_b, st_b = _delta_chunk(True, nh, hd, *in_b, sb_ref)
    of_ref[...] = out_f
    ob_ref[...] = out_b
    for h in range(nh):
        sf_ref[h] = st_f[h]
        sb_ref[h] = st_b[h]


def _delta_chunk(reverse, nh, hd, q_ref, k_ref, v_ref, gcol_ref, grow_ref, s_ref):
    cs = q_ref.shape[0]
    r = lax.broadcasted_iota(I32, (cs, 2 * cs), 0)
    lane = lax.broadcasted_iota(I32, (cs, 2 * cs), 1)
    left = lane < cs
    cc = jnp.where(left, lane, lane - cs)
    ahead = (cc - r) if reverse else (r - cc)
    tri = ahead >= 0
    strict = ahead > 0
    eye = (r == cc).astype(F32)
    left_w = lax.broadcasted_iota(I32, (cs, 2 * hd), 1) < hd
    dot = functools.partial(jnp.dot, preferred_element_type=F32)

    def block_diag(x, is_left):
        zero = jnp.zeros_like(x)
        return jnp.concatenate([jnp.where(is_left, x, zero), jnp.where(is_left, zero, x)], axis=0)

    nt_dims = (((1,), (1,)), ((), ()))
    tn_dims = (((0,), (0,)), ((), ()))
    n_sq = max(1, (cs - 1).bit_length() - 1)
    pairs = list(range(0, nh, 2))
    heads = list(range(nh))
    cols = [slice(h * hd, (h + 1) * hd) for h in heads]
    kf = [k_ref[:, s_].astype(F32) for s_ in cols]
    gc = [gcol_ref[:, h:h + 1] for h in heads]
    beta = [gcol_ref[:, nh + h:nh + h + 1] for h in heads]
    gr = [gcol_ref[:, 2 * nh + h:2 * nh + h + 1] for h in heads]
    s_old = [s_ref[h] for h in heads]
    sb = [a.astype(BF16) for a in s_old]
    eg = [jnp.exp(a) for a in gc]
    qks = [dot(jnp.concatenate([(q_ref[:, s_].astype(F32) * e).astype(BF16), (c * (b * e)).astype(BF16)], axis=0), st)
           for s_, c, b, e, st in zip(cols, kf, beta, eg, sb)]
    cols2 = [slice(h * hd, (h + 2) * hd) for h in pairs]
    k2 = [k_ref[:, s_] for s_ in cols2]
    q2 = [q_ref[:, s_] for s_ in cols2]
    gc2 = [jnp.where(left, gc[h], gc[h + 1]) for h in pairs]
    beta2 = [jnp.where(left, beta[h], beta[h + 1]) for h in pairs]
    gc_row2 = [jnp.concatenate([grow_ref[h:h + 1, :], grow_ref[h + 1:h + 2, :]], axis=1) for h in pairs]
    decay = [jnp.exp(jnp.where(tri, a - b, -jnp.inf)) for a, b in zip(gc2, gc_row2)]
    kq = [lax.dot_general(jnp.concatenate([a, b], axis=0), block_diag(a, left_w), nt_dims,
                          preferred_element_type=F32) for a, b in zip(k2, q2)]
    low = [jnp.where(strict, b * a[:cs] * dcy, 0.0) for a, b, dcy in zip(kq, beta2, decay)]
    attn = [(a[cs:] * dcy).astype(BF16) for a, dcy in zip(kq, decay)]
    inv = [eye - a for a in low]
    x = [dot(a.astype(BF16), block_diag(a, left).astype(BF16)) for a in low]
    for j in range(n_sq):
        w = [block_diag(a, left).astype(BF16) for a in x]
        if j + 1 < n_sq:
            both = [dot(jnp.concatenate([p, a], axis=0).astype(BF16), b) for p, a, b in zip(inv, x, w)]
            inv = [p + a[:cs] for p, a in zip(inv, both)]
            x = [a[cs:] for a in both]
        else:
            inv = [p + dot(p.astype(BF16), b) for p, b in zip(inv, w)]
    rhs = [jnp.concatenate([v_ref[:, cols[h]].astype(F32) * beta[h] - qks[h][cs:],
                            v_ref[:, cols[h + 1]].astype(F32) * beta[h + 1] - qks[h + 1][cs:]], axis=1)
           for h in pairs]
    v_new = [dot(p.astype(BF16), block_diag(a, left_w).astype(BF16)) for p, a in zip(inv, rhs)]
    vb = [a.astype(BF16) for a in v_new]
    outs = [jnp.concatenate([qks[h][:cs], qks[h + 1][:cs]], axis=1) + dot(a, block_diag(b, left_w))
            for h, a, b in zip(pairs, attn, vb)]
    kd = [(a * jnp.exp(b)).astype(BF16) for a, b in zip(kf, gr)]
    g_tot = [jnp.exp(a[0:1, :] + b[0:1, :]) for a, b in zip(gc, gr)]
    states = [s_old[h] * g_tot[h]
              + lax.dot_general(kd[h], vb[h // 2][:, (h % 2) * hd:(h % 2 + 1) * hd], tn_dims,
                                preferred_element_type=F32) for h in heads]
    return jnp.concatenate(outs, axis=1), states


def delta_rule(qkv, gcol, grow, nh, hd, seq_a, n_a):
    t = qkv.shape[0]
    width = nh * hd
    cs = DN_CHUNK
    n_chunks = t // cs
    ng = gcol.shape[-1]
    assert nh % 2 == 0 and 2 * cs == LANES
    grow = grow.reshape(2, ng, n_chunks, cs).transpose(0, 2, 1, 3)

    def specs(d):
        chunk = (lambda i: i) if d == 0 else (lambda i: n_chunks - 1 - i)
        return [pl.BlockSpec((cs, width), lambda i: (chunk(i), 0)),
                pl.BlockSpec((cs, width), lambda i: (chunk(i), 1)),
                pl.BlockSpec((cs, width), lambda i: (chunk(i), 2)),
                pl.BlockSpec((None, cs, ng), lambda i: (d, chunk(i), 0)),
                pl.BlockSpec((None, None, ng, cs), lambda i: (d, chunk(i), 0, 0))]

    out = jax.ShapeDtypeStruct((t, width), F32)
    return pl.pallas_call(
        functools.partial(_delta_kernel, seq_a // cs, n_a // cs, n_chunks, nh, hd),
        grid=(n_chunks,),
        in_specs=specs(0) + specs(1),
        out_specs=[pl.BlockSpec((cs, width), lambda i: (i, 0)),
                   pl.BlockSpec((cs, width), lambda i: (n_chunks - 1 - i, 0))],
        out_shape=[out, out],
        scratch_shapes=[pltpu.VMEM((nh, hd, hd), F32), pltpu.VMEM((nh, hd, hd), F32)],
        compiler_params=_params(("arbitrary",)),
        name="delta_rule",
    )(qkv, qkv, qkv, gcol, grow, qkv, qkv, qkv, gcol, grow)


def _dn_post_kernel(hd, col_block, of_ref, ob_ref, og_ref, nw_ref, y_ref):
    del col_block
    o = of_ref[...] + ob_ref[...]
    og = og_ref[...].astype(F32)
    nw = nw_ref[...]
    for h in range(o.shape[1] // hd):
        hs = slice(h * hd, (h + 1) * hd)
        oh = o[:, hs]
        inv = lax.rsqrt(jnp.mean(oh * oh, -1, keepdims=True) + RMS_EPS)
        y_ref[:, hs] = (oh * inv * nw * _silu(og[:, hs])).astype(y_ref.dtype)


def dn_post(o_f, o_b, z, og_block, norm_w, hd, tm=256):
    t, width = o_f.shape
    return pl.pallas_call(
        functools.partial(_dn_post_kernel, hd, og_block),
        grid=(t // tm,),
        in_specs=[pl.BlockSpec((tm, width), lambda i: (i, 0)),
                  pl.BlockSpec((tm, width), lambda i: (i, 0)),
                  pl.BlockSpec((tm, width), lambda i: (i, og_block)),
                  pl.BlockSpec((1, hd), lambda i: (0, 0))],
        out_specs=pl.BlockSpec((tm, width), lambda i: (i, 0)),
        out_shape=jax.ShapeDtypeStruct((t, width), BF16),
        compiler_params=_params(("arbitrary",)),
        name="dn_post",
    )(o_f, o_b, z, norm_w.reshape(1, hd).astype(F32))


def _sc_kernel(seq_a, n_a, total, b_ref, c_ref, cprev_ref, cnext_ref, h_ref, hprev_ref, hnext_ref, w_ref, y_ref):
    i = pl.program_id(0)
    tm = b_ref.shape[0]
    starts, ends = _seq_edges(i * tm, tm, seq_a, n_a, total)
    x = c_ref[...].astype(F32) * h_ref[...].astype(F32)
    prev_row = jnp.where(starts, 0.0, cprev_ref[HALO - 1:HALO, :].astype(F32) * hprev_ref[HALO - 1:HALO, :].astype(F32))
    next_row = jnp.where(ends, 0.0, cnext_ref[0:1, :].astype(F32) * hnext_ref[0:1, :].astype(F32))
    xm, xp = _shifted(x, prev_row, next_row)
    conv = xm * w_ref[0:1, :] + x * w_ref[1:2, :] + xp * w_ref[2:3, :]
    y_ref[...] = (b_ref[...].astype(F32) * conv).astype(y_ref.dtype)


def short_conv(z, w_conv, first_block, width, seq_a, n_a, tm=256):
    t = z.shape[0]
    return pl.pallas_call(
        functools.partial(_sc_kernel, seq_a, n_a, t),
        grid=(t // tm,),
        in_specs=([pl.BlockSpec((tm, width), lambda i: (i, first_block))]
                  + _halo_specs(tm, width, lambda i: first_block + 1, t)
                  + _halo_specs(tm, width, lambda i: first_block + 2, t)
                  + [pl.BlockSpec((3, width), lambda i: (0, 0))]),
        out_specs=pl.BlockSpec((tm, width), lambda i: (i, 0)),
        out_shape=jax.ShapeDtypeStruct((t, width), BF16),
        compiler_params=_params(("arbitrary",)),
        name="short_conv",
    )(z, z, z, z, z, z, z, w_conv)


def _xattn_kernel(scale, q_ref, k_ref, v_ref, o_ref):
    s = lax.dot_general(q_ref[...], k_ref[...], (((1,), (1,)), ((), ())), preferred_element_type=F32) * scale
    m = jnp.max(s, -1, keepdims=True)
    e = jnp.exp(s - m)
    p = e / jnp.sum(e, -1, keepdims=True)
    o_ref[...] = jnp.dot(p.astype(BF16), v_ref[...], preferred_element_type=F32).astype(o_ref.dtype)


def mem_attention(z, kv, q_block0, n_heads, hd, n_mem, seq_a, n_a, tm=512):
    t = z.shape[0]
    n_seq_a = n_a // seq_a

    def seq_of(i):
        tok = i * tm
        return jnp.where(tok < n_a, tok // seq_a, n_seq_a)

    return pl.pallas_call(
        functools.partial(_xattn_kernel, hd ** -0.5),
        grid=(t // tm, n_heads),
        in_specs=[pl.BlockSpec((tm, hd), lambda i, h: (i, q_block0 + h)),
                  pl.BlockSpec((n_mem, hd), lambda i, h: (seq_of(i), h)),
                  pl.BlockSpec((n_mem, hd), lambda i, h: (seq_of(i), n_heads + h))],
        out_specs=pl.BlockSpec((tm, hd), lambda i, h: (i, h)),
        out_shape=jax.ShapeDtypeStruct((t, n_heads * hd), BF16),
        compiler_params=_params(("arbitrary", "arbitrary")),
        name="mem_attention",
    )(z, kv, kv)


def _merge_kernel(nb, *refs):
    y_refs = refs[:nb]
    g_refs = refs[nb:2 * nb]
    w_ref = refs[2 * nb]
    o_ref = refs[2 * nb + 1]
    acc = None
    for b in range(nb):
        term = _sigmoid(g_refs[b][...].astype(F32)) * jnp.dot(y_refs[b][...], w_ref[b], preferred_element_type=F32)
        acc = term if acc is None else acc + term
    o_ref[...] = acc.astype(o_ref.dtype)


def gated_merge(ys, z, gate_col0, w_branch, tm=1024, tn=512):
    nb = len(ys)
    t, width = ys[0].shape
    d = w_branch.shape[-1]
    g0 = gate_col0 // tn
    per = d // tn
    gate_spec = lambda b: pl.BlockSpec((tm, tn), lambda i, j: (i, g0 + b * per + j))
    return pl.pallas_call(
        functools.partial(_merge_kernel, nb),
        grid=(t // tm, d // tn),
        in_specs=([pl.BlockSpec((tm, width), lambda i, j: (i, 0))] * nb
                  + [gate_spec(b) for b in range(nb)]
                  + [pl.BlockSpec((nb, width, tn), lambda i, j: (0, 0, j))]),
        out_specs=pl.BlockSpec((tm, tn), lambda i, j: (i, j)),
        out_shape=jax.ShapeDtypeStruct((t, d), BF16),
        compiler_params=_params(("arbitrary", "arbitrary")),
        name="gated_merge",
    )(*ys, *([z] * nb), w_branch)


def _res_ln1_kernel(x_ref, y_ref, g_ref, b_ref, of_ref, op_ref):
    h = _layer_norm(DEEPNORM_ALPHA * x_ref[...] + y_ref[...], g_ref[...], b_ref[...])
    of_ref[...] = h
    half = h.shape[1] // 2
    op_ref[...] = _pack_bf16_pair(h[:, :half], h[:, half:])


def res_ln1(x, y, g, b, tm=256):
    t, d = x.shape
    row = lambda w: pl.BlockSpec((tm, w), lambda i: (i, 0))
    vec = pl.BlockSpec((1, d), lambda i: (0, 0))
    return pl.pallas_call(
        _res_ln1_kernel,
        grid=(t // tm,),
        in_specs=[row(d), row(d), vec, vec],
        out_specs=[row(d), row(d // 2)],
        out_shape=[jax.ShapeDtypeStruct((t, d), F32), jax.ShapeDtypeStruct((t, d // 2), U32)],
        compiler_params=_params(("arbitrary",)),
        name="res_ln1",
    )(x, y, g.reshape(1, d), b.reshape(1, d))


def _first_index_of_max(x, idx, big):
    m = jnp.max(x, axis=0, keepdims=True)
    return m, jnp.min(jnp.where(x == m, idx, big), axis=0, keepdims=True)


def _router_kernel(ne, ng, kg, topk, h_ref, wt_ref, bias_ref, eidx_ref, gate_ref, rank_ref, cnt_ref, run_ref):
    i = pl.program_id(0)
    tm = h_ref.shape[0]
    per = ne // ng

    @pl.when(i == 0)
    def _():
        run_ref[...] = jnp.zeros_like(run_ref)

    logits = lax.dot_general(wt_ref[...], h_ref[...], (((1,), (1,)), ((), ())),
                             preferred_element_type=F32, precision=lax.Precision.HIGHEST)
    scores = _sigmoid(logits)
    sel = scores + bias_ref[...]
    eid = lax.broadcasted_iota(I32, (ne, tm), 0)
    neg = -jnp.inf
    pid = lax.broadcasted_iota(I32, (per, tm), 0)
    gscores = []
    for g in range(ng):
        sg = sel[g * per:(g + 1) * per, :]
        m1, a1 = _first_index_of_max(sg, pid, per)
        m2 = jnp.max(jnp.where(pid == a1, neg, sg), axis=0, keepdims=True)
        gscores.append(m1 + m2)
    gscore = jnp.concatenate(gscores, axis=0)
    gid = lax.broadcasted_iota(I32, (ng, tm), 0)
    egroup = eid // per
    keep_e = jnp.zeros((ne, tm), jnp.bool_)
    for _ in range(kg):
        _, a = _first_index_of_max(gscore, gid, ng)
        keep_e = keep_e | (egroup == a)
        gscore = jnp.where(gid == a, neg, gscore)
    cand = jnp.where(keep_e, sel, neg)
    onehot = jnp.zeros((ne, tm), F32)
    idxs, gates = [], []
    for _ in range(topk):
        _, a = _first_index_of_max(cand, eid, ne)
        hit = eid == a
        idxs.append(a)
        gates.append(jnp.sum(jnp.where(hit, scores, 0.0), axis=0, keepdims=True))
        onehot = jnp.where(hit, 1.0, onehot)
        cand = jnp.where(hit, neg, cand)
    gate = jnp.concatenate(gates, axis=0)
    gate = gate / jnp.sum(gate, axis=0, keepdims=True) * ROUTE_SCALE
    s_ = lax.broadcasted_iota(I32, (tm, tm), 0)
    t_ = lax.broadcasted_iota(I32, (tm, tm), 1)
    before = jnp.where(s_ < t_, 1.0, 0.0).astype(BF16)
    prior = jnp.dot(onehot.astype(BF16), before, preferred_element_type=F32) + run_ref[...]
    ranks = [jnp.sum(jnp.where(eid == a, prior, 0.0), axis=0, keepdims=True) for a in idxs]
    eidx_ref[...] = jnp.concatenate(idxs, axis=0)
    gate_ref[...] = gate
    rank_ref[...] = jnp.concatenate(ranks, axis=0).astype(I32)
    run_ref[...] = run_ref[...] + jnp.sum(onehot, axis=1, keepdims=True)
    cnt_ref[...] = run_ref[...].astype(I32)


def router(h_f32, w_router, bias, tm=256):
    t, d = h_f32.shape
    ne = w_router.shape[1]
    out = lambda dt: jax.ShapeDtypeStruct((TOP_K, t), dt)
    tok = pl.BlockSpec((TOP_K, tm), lambda i: (0, i))
    return pl.pallas_call(
        functools.partial(_router_kernel, ne, N_EXPERT_GROUPS, TOPK_GROUPS, TOP_K),
        grid=(t // tm,),
        in_specs=[pl.BlockSpec((tm, d), lambda i: (i, 0)), pl.BlockSpec((ne, d), lambda i: (0, 0)),
                  pl.BlockSpec((ne, 1), lambda i: (0, 0))],
        out_specs=[tok, tok, tok, pl.BlockSpec((ne, 1), lambda i: (0, 0))],
        out_shape=[out(I32), out(F32), out(I32), jax.ShapeDtypeStruct((ne, 1), I32)],
        scratch_shapes=[pltpu.VMEM((ne, 1), F32)],
        compiler_params=_params(("arbitrary",)),
        name="router",
    )(h_f32, w_router.T.astype(F32), bias.reshape(ne, 1).astype(F32))


def _zero_blocks_kernel(blk_ref, o_ref):
    del blk_ref
    o_ref[...] = jnp.zeros_like(o_ref)


def zero_blocks(block_ids, n_rows, w, tm):
    return pl.pallas_call(
        _zero_blocks_kernel,
        grid_spec=pltpu.PrefetchScalarGridSpec(
            num_scalar_prefetch=1,
            grid=(block_ids.shape[0],),
            in_specs=[],
            out_specs=pl.BlockSpec((tm, w), lambda e, blk: (blk[e], 0)),
        ),
        out_shape=jax.ShapeDtypeStruct((n_rows, w), U32),
        compiler_params=_params(("arbitrary",)),
        name="moe_pad_zero",
    )(block_ids)


def _dispatch_kernel(topk, pos_ref, h_ref, dst_in_ref, dst_ref, sem):
    del dst_in_ref
    tb = h_ref.shape[0]

    def issue(r, carry):
        for k in range(topk):
            pltpu.make_async_copy(h_ref.at[pl.ds(r, 1), :], dst_ref.at[pl.ds(pos_ref[0, k, r], 1), :], sem).start()
        return carry

    lax.fori_loop(0, tb, issue, 0, unroll=True)
    for k in range(topk):
        pltpu.make_async_copy(h_ref, dst_ref.at[pl.ds(0, tb), :], sem).wait()


def moe_dispatch(h_pk, pos3, dst, tb):
    t, w = h_pk.shape
    nb, topk, _ = pos3.shape
    return pl.pallas_call(
        functools.partial(_dispatch_kernel, topk),
        grid=(nb,),
        in_specs=[pl.BlockSpec((1, topk, tb), lambda i: (i, 0, 0), memory_space=pltpu.SMEM),
                  pl.BlockSpec((tb, w), lambda i: (i, 0)),
                  pl.BlockSpec(memory_space=pl.ANY)],
        out_specs=pl.BlockSpec(memory_space=pl.ANY),
        out_shape=jax.ShapeDtypeStruct(dst.shape, dst.dtype),
        input_output_aliases={2: 0},
        scratch_shapes=[pltpu.SemaphoreType.DMA(())],
        compiler_params=_params(("arbitrary",)),
        name="moe_dispatch",
    )(pos3, h_pk, dst)


GMM_K_CHUNKS = 4


def _gmm_up_kernel(be_ref, nb_ref, x_ref, wg_ref, wu_ref, o_ref):
    b = pl.program_id(1)

    @pl.when(b < nb_ref[0])
    def _():
        words = x_ref.shape[1]
        kc = 2 * words // GMM_K_CHUNKS
        per_half = GMM_K_CHUNKS // 2
        g = u = None
        for c in range(GMM_K_CHUNKS):
            w = x_ref[:, (c % per_half) * kc:(c % per_half + 1) * kc]
            bits = (w << 16) if c < per_half else (w & jnp.uint32(0xFFFF0000))
            xc = lax.bitcast_convert_type(bits, F32).astype(BF16)
            rows = slice(c * kc, (c + 1) * kc)
            gc = jnp.dot(xc, wg_ref[rows, :].astype(BF16), preferred_element_type=F32)
            uc = jnp.dot(xc, wu_ref[rows, :].astype(BF16), preferred_element_type=F32)
            g = gc if g is None else g + gc
            u = uc if u is None else u + uc
        o_ref[...] = (_silu(g) * u).astype(o_ref.dtype)

    @pl.when(b >= nb_ref[0])
    def _():
        o_ref[...] = jnp.zeros_like(o_ref)


def gmm_up(x, w_gate, w_up, block_expert, n_used, tm, name, tn=512):
    rows, xw = x.shape
    _, d, f = w_gate.shape
    tn = _tile(f, tn)
    assert d == 2 * xw and (d // GMM_K_CHUNKS) % LANES == 0
    wspec = pl.BlockSpec((None, d, tn), lambda j, b, be, nb: (be[b], 0, j))
    return pl.pallas_call(
        _gmm_up_kernel,
        grid_spec=pltpu.PrefetchScalarGridSpec(
            num_scalar_prefetch=2,
            grid=(f // tn, rows // tm),
            in_specs=[pl.BlockSpec((tm, xw), lambda j, b, be, nb: (jnp.minimum(b, nb[0] - 1), 0)), wspec, wspec],
            out_specs=pl.BlockSpec((tm, tn), lambda j, b, be, nb: (b, j)),
        ),
        out_shape=jax.ShapeDtypeStruct((rows, f), BF16),
        compiler_params=_params(("arbitrary", "arbitrary")),
        name=name,
    )(block_expert, n_used, x, w_gate, w_up)


def _gmm_down_kernel(n_ranges, be_ref, nb_ref, h_ref, w_ref, o_ref):
    b = pl.program_id(1)

    @pl.when(b < nb_ref[0])
    def _():
        h = h_ref[...]
        cw = w_ref.shape[1] // n_ranges
        for c in range(n_ranges):
            y = jnp.dot(h, w_ref[:, c * cw:(c + 1) * cw].astype(BF16), preferred_element_type=F32)
            o_ref[:, c * (cw // 2):(c + 1) * (cw // 2)] = _pack_bf16_pair(y[:, :cw // 2], y[:, cw // 2:])

    @pl.when(b >= nb_ref[0])
    def _():
        o_ref[...] = jnp.zeros_like(o_ref)


def down_ranges(d):
    return max(r for r in (4, 2, 1) if (d // r) % (2 * LANES) == 0)


def gmm_down(h, w_down, block_expert, n_used, tm, name):
    rows, f = h.shape
    _, _, d = w_down.shape
    return pl.pallas_call(
        functools.partial(_gmm_down_kernel, down_ranges(d)),
        grid_spec=pltpu.PrefetchScalarGridSpec(
            num_scalar_prefetch=2,
            grid=(1, rows // tm),
            in_specs=[pl.BlockSpec((tm, f), lambda j, b, be, nb: (jnp.minimum(b, nb[0] - 1), 0)),
                      pl.BlockSpec((None, f, d), lambda j, b, be, nb: (be[b], 0, 0))],
            out_specs=pl.BlockSpec((tm, d // 2), lambda j, b, be, nb: (b, 0)),
        ),
        out_shape=jax.ShapeDtypeStruct((rows, d // 2), U32),
        compiler_params=_params(("arbitrary", "arbitrary")),
        name=name,
    )(block_expert, n_used, h, w_down)


def _unpack_rows(w, n_split):
    q = w.shape[1] // n_split
    parts = []
    for s in range(n_split):
        lo, hi = _unpack_bf16_pair(w[:, s * q:(s + 1) * q])
        parts += [lo, hi]
    return jnp.concatenate(parts, axis=1)


COMBINE_DEPTH = 3


def _combine_kernel(topk, n_split, half_blocks, pos_ref, pos1_ref, pos2_ref, y_ref, h_ref, sh_ref, gate_ref, g_ref,
                    b_ref, op_ref, os_ref, *scratch):
    bufs, sem = scratch[:COMBINE_DEPTH], scratch[COMBINE_DEPTH]
    i = pl.program_id(0)
    n = pl.num_programs(0)
    tb = h_ref.shape[0]

    def issue(p_ref, s):
        def body(r, carry):
            for k in range(topk):
                pltpu.make_async_copy(y_ref.at[pl.ds(p_ref[0, k, r], 1), :], bufs[s].at[k, pl.ds(r, 1), :],
                                      sem.at[s]).start()
            return carry
        lax.fori_loop(0, tb, body, 0, unroll=True)

    def drain(s):
        for k in range(topk):
            pltpu.make_async_copy(y_ref.at[pl.ds(0, tb), :], bufs[s].at[k], sem.at[s]).wait()

    @pl.when(i == 0)
    def _():
        issue(pos_ref, 0)
        issue(pos1_ref, 1)

    def step(s):
        drain(s)
        issue(pos2_ref, (s + 2) % COMBINE_DEPTH)
        acc = DEEPNORM_ALPHA * h_ref[...] + _unpack_rows(sh_ref[...], n_split)
        for k in range(topk):
            acc = acc + gate_ref[:, k:k + 1] * _unpack_rows(bufs[s][k], n_split)
        out = _layer_norm(acc, g_ref[...], b_ref[...])

        @pl.when(i < half_blocks)
        def _():
            op_ref[...] = out

        @pl.when(i >= half_blocks)
        def _():
            os_ref[...] = out

        @pl.when(i == n - 1)
        def _():
            drain((s + 1) % COMBINE_DEPTH)
            drain((s + 2) % COMBINE_DEPTH)

    for s in range(COMBINE_DEPTH):
        pl.when(i % COMBINE_DEPTH == s)(functools.partial(step, s))


def moe_combine(y_sorted, pos3, gate, h_f32, shared, g, b, n_first, n_split):
    t, d = h_f32.shape
    nb, topk, tb = pos3.shape
    hb = n_first // tb
    smem = lambda f: pl.BlockSpec((1, topk, tb), f, memory_space=pltpu.SMEM)
    row = lambda w: pl.BlockSpec((tb, w), lambda i: (i, 0))
    vec = pl.BlockSpec((1, d), lambda i: (0, 0))
    return pl.pallas_call(
        functools.partial(_combine_kernel, topk, n_split, hb),
        grid=(nb,),
        in_specs=[smem(lambda i: (i, 0, 0)), smem(lambda i: (jnp.minimum(i + 1, nb - 1), 0, 0)),
                  smem(lambda i: (jnp.minimum(i + 2, nb - 1), 0, 0)),
                  pl.BlockSpec(memory_space=pl.ANY), row(d), row(d // 2), row(topk), vec, vec],
        out_specs=[pl.BlockSpec((tb, d), lambda i: (jnp.minimum(i, hb - 1), 0)),
                   pl.BlockSpec((tb, d), lambda i: (jnp.maximum(i - hb, 0), 0))],
        out_shape=[jax.ShapeDtypeStruct((n_first, d), F32), jax.ShapeDtypeStruct((t - n_first, d), F32)],
        scratch_shapes=([pltpu.VMEM((topk, tb, d // 2), U32)] * COMBINE_DEPTH
                        + [pltpu.SemaphoreType.DMA((COMBINE_DEPTH,))]),
        compiler_params=_params(("arbitrary",)),
        name="moe_combine",
    )(pos3, pos3, pos3, y_sorted, h_f32, shared, gate, g.reshape(1, d), b.reshape(1, d))


MOE_TM = 512
MOE_TB = 64


def _moe_schedule(eidx, rank, counts, tm, tb):
    topk, t = eidx.shape
    ne = counts.shape[0]
    n_blocks = (t * topk + ne * (tm - 1) + tm - 1) // tm
    padded = (counts + tm - 1) // tm * tm
    ends = jnp.cumsum(padded)
    starts = ends - padded
    start_of = jnp.sum(jnp.where(eidx[:, :, None] == jnp.arange(ne, dtype=I32), starts.astype(I32), 0), -1)
    pos = (start_of + rank).astype(I32)
    pos3 = pos.reshape(topk, t // tb, tb).transpose(1, 0, 2)
    n_used = (ends[-1] // tm).astype(I32)
    blk = jnp.arange(n_blocks, dtype=I32)
    blk_e = jnp.searchsorted(ends, jnp.minimum(blk, n_used - 1) * tm, side="right").astype(I32)
    blk_e = jnp.minimum(blk_e, ne - 1)
    pad_blk = jnp.where(counts > 0, ends // tm - 1, jnp.minimum(starts // tm, n_blocks - 1)).astype(I32)
    return pos3, blk_e, n_used.reshape(1), pad_blk, n_blocks * tm


def kernel(x_prompt, x_sample, mem_prompt, mem_sample, ln_in_g, ln_in_b, w_in, w_dn_conv, dn_a_log, dn_dt_bias,
           dn_norm_w, w_sc_conv, w_mem_kv, w_branch, w_o, ln1_g, ln1_b, w_router, router_bias, w_gate_e, w_up_e,
           w_down_e, w_gate_s, w_up_s, w_down_s, ln2_g, ln2_b):
    assert w_in.shape[0] == DEPTH == 1
    bp, sp, d = x_prompt.shape
    bs, ss, _ = x_sample.shape
    assert bs == 1
    n_a, seq_a = bp * sp, sp
    t = n_a + bs * ss
    nh = dn_a_log.shape[-1]
    hd = dn_norm_w.shape[-1]
    dn_w = nh * hd
    sc_w = w_sc_conv.shape[-1]
    x_w = w_mem_kv.shape[-1] // 2
    n_mem = mem_prompt.shape[1]
    nbr = w_branch.shape[1]
    assert dn_w == sc_w == x_w, "column blocks of the combined projection are addressed in units of one width"
    blk = dn_w
    lyr = 0

    ab0 = 4 * dn_w
    w_main, w_cut = drop_columns(w_in, lyr, ab0, 4 * nh)
    w_ab = w_cut[:, :4 * nh]

    h0_f, h0_b = ln_in(x_prompt.reshape(n_a, d), x_sample.reshape(t - n_a, d), ln_in_g, ln_in_b)
    z = matmul(h0_b, w_main, 1024, 1024, BF16, "in_proj")

    gcol, grow = dn_gates(h0_b, w_ab, dn_a_log[lyr], dn_dt_bias[lyr], nh, DN_CHUNK)
    qkv = dn_prep(z, w_dn_conv[lyr].astype(F32), dn_w, hd, seq_a, n_a)
    o_f, o_b = delta_rule(qkv, gcol, grow, nh, hd, seq_a, n_a)
    y_dn = dn_post(o_f, o_b, z, 3, dn_norm_w[lyr], hd)
    y_sc = short_conv(z, w_sc_conv[lyr].astype(F32), 4, sc_w, seq_a, n_a)
    mem = jnp.concatenate([mem_prompt.reshape(bp * n_mem, d), mem_sample.reshape(bs * n_mem, d)], 0).astype(BF16)
    kv = matmul(mem, w_mem_kv[lyr].astype(BF16), 256, 1024, BF16, "mem_kv")
    xhd = x_w // N_X_HEADS
    y_mem = mem_attention(z, kv, 7 * blk // xhd, N_X_HEADS, xhd, n_mem, seq_a, n_a)

    mixed = gated_merge([y_dn, y_sc, y_mem], z, 8 * blk, w_branch[lyr].astype(BF16))
    attn_out = matmul(mixed, w_o[lyr].astype(BF16), 1024, 1024, F32, "out_proj")
    h_f, h_pk = res_ln1(h0_f, attn_out, ln1_g[lyr], ln1_b[lyr])

    eidx, gate, rank, counts = router(h_f, w_router[lyr], router_bias[lyr])
    pos3, blk_e, n_used, pad_blk, n_rows = _moe_schedule(eidx, rank, counts.reshape(-1), MOE_TM, MOE_TB)
    x_sorted = moe_dispatch(h_pk, pos3, zero_blocks(pad_blk, n_rows, d // 2, MOE_TM), MOE_TB)
    hid = gmm_up(x_sorted, w_gate_e[lyr], w_up_e[lyr], blk_e, n_used, MOE_TM, "moe_up")
    y_sorted = gmm_down(hid, w_down_e[lyr], blk_e, n_used, MOE_TM, "moe_down")
    one = jnp.zeros((t // MOE_TM,), I32)
    all_blocks = jnp.full((1,), t // MOE_TM, I32)
    hid_s = gmm_up(h_pk, w_gate_s, w_up_s, one, all_blocks, MOE_TM, "shared_up")
    y_shared = gmm_down(hid_s, w_down_s, one, all_blocks, MOE_TM, "shared_down")
    y_p, y_s = moe_combine(y_sorted, pos3, gate.T, h_f, y_shared, ln2_g[lyr], ln2_b[lyr], n_a, down_ranges(d))
    return y_p.reshape(bp, sp, d), y_s.reshape(bs, ss, d)
```

```python
import functools

import jax
import jax.numpy as jnp
from jax import lax
from jax.experimental import pallas as pl
from jax.experimental.pallas import tpu as pltpu

F32 = jnp.float32
BF16 = jnp.bfloat16
I32 = jnp.int32
U32 = jnp.uint32

DN_CHUNK = 64
N_X_HEADS = 4
N_EXPERT_GROUPS = 8
TOPK_GROUPS = 4
TOP_K = 8
ROUTE_SCALE = 2.5
DEPTH = 1
DEEPNORM_ALPHA = (2 * DEPTH) ** 0.25
LN_EPS = 1e-5
RMS_EPS = 1e-6
L2_EPS = 1e-6

V7X_VMEM_BYTES = 64 * 1024 * 1024
VMEM_LIMIT = 56 * 1024 * 1024
LANES = 128
HALO = 16


def _params(sem):
    return pltpu.CompilerParams(dimension_semantics=sem, vmem_limit_bytes=VMEM_LIMIT)


def _sigmoid(x):
    return 1.0 / (1.0 + jnp.exp(-x))


def _silu(x):
    return x * _sigmoid(x)


def _softplus(x):
    return jnp.maximum(x, 0.0) + jnp.log1p(jnp.exp(-jnp.abs(x)))


def _tile(n, pref):
    while n % pref:
        pref //= 2
    return pref


def _layer_norm(x, g, b):
    mu = jnp.mean(x, -1, keepdims=True)
    xc = x - mu
    var = jnp.mean(xc * xc, -1, keepdims=True)
    return xc * lax.rsqrt(var + LN_EPS) * g + b


def _pack_bf16_pair(lo, hi):
    lo_b = lax.bitcast_convert_type(lo.astype(BF16).astype(F32), U32)
    hi_b = lax.bitcast_convert_type(hi.astype(BF16).astype(F32), U32)
    return (hi_b & jnp.uint32(0xFFFF0000)) | (lo_b >> 16)


def _unpack_bf16_pair(w):
    lo = lax.bitcast_convert_type(w << 16, F32)
    hi = lax.bitcast_convert_type(w & jnp.uint32(0xFFFF0000), F32)
    return lo, hi


def _ln_in_kernel(half_blocks, xp_ref, xs_ref, g_ref, b_ref, of_ref, ob_ref):
    i = pl.program_id(0)

    def emit(x_ref):
        y = _layer_norm(x_ref[...], g_ref[...], b_ref[...])
        of_ref[...] = y
        ob_ref[...] = y.astype(BF16)

    @pl.when(i < half_blocks)
    def _():
        emit(xp_ref)

    @pl.when(i >= half_blocks)
    def _():
        emit(xs_ref)


def ln_in(xp, xs, g, b, tm=256):
    tp, d = xp.shape
    ts = xs.shape[0]
    hb = tp // tm
    nb = hb + ts // tm
    return pl.pallas_call(
        functools.partial(_ln_in_kernel, hb),
        grid=(nb,),
        in_specs=[
            pl.BlockSpec((tm, d), lambda i: (jnp.minimum(i, hb - 1), 0)),
            pl.BlockSpec((tm, d), lambda i: (jnp.maximum(i - hb, 0), 0)),
            pl.BlockSpec((1, d), lambda i: (0, 0)),
            pl.BlockSpec((1, d), lambda i: (0, 0)),
        ],
        out_specs=[pl.BlockSpec((tm, d), lambda i: (i, 0)), pl.BlockSpec((tm, d), lambda i: (i, 0))],
        out_shape=[jax.ShapeDtypeStruct((tp + ts, d), F32), jax.ShapeDtypeStruct((tp + ts, d), BF16)],
        compiler_params=_params(("arbitrary",)),
        name="ln_in",
    )(xp, xs, g.reshape(1, d), b.reshape(1, d))


def _mm_kernel(a_ref, b_ref, o_ref):
    o_ref[...] = jnp.dot(a_ref[...], b_ref[...], preferred_element_type=F32).astype(o_ref.dtype)


def matmul(a, b, tm, tn, out_dtype, name):
    m, k = a.shape
    n = b.shape[1]
    tm, tn = _tile(m, tm), _tile(n, tn)
    return pl.pallas_call(
        _mm_kernel,
        grid=(m // tm, n // tn),
        in_specs=[pl.BlockSpec((tm, k), lambda i, j: (i, 0)), pl.BlockSpec((k, tn), lambda i, j: (0, j))],
        out_specs=pl.BlockSpec((tm, tn), lambda i, j: (i, j)),
        out_shape=jax.ShapeDtypeStruct((m, n), out_dtype),
        compiler_params=_params(("arbitrary", "arbitrary")),
        name=name,
    )(a, b)


def _drop_columns_kernel(n_aligned, gap, a_ref, b_ref, o_ref, cut_ref):
    j = pl.program_id(1)

    @pl.when(j < n_aligned)
    def _():
        o_ref[...] = a_ref[...].T.astype(o_ref.dtype)

    @pl.when(j >= n_aligned)
    def _():
        o_ref[...] = jnp.concatenate([a_ref[gap:, :], b_ref[...]], axis=0).T.astype(o_ref.dtype)

    @pl.when(j == n_aligned)
    def _():
        cut_ref[...] = a_ref[:LANES, :].T.astype(cut_ref.dtype)


def drop_columns(w, layer, col0, gap, tr=512, tile=1024):
    wt = jnp.swapaxes(w[layer], 0, 1)
    n_in, rows = wt.shape
    n_out = n_in - gap
    tile = _tile(n_out, tile)
    tr = _tile(rows, tr)
    assert col0 % tile == 0 and tile % gap == 0 and gap % 8 == 0 and gap < LANES
    return pl.pallas_call(
        functools.partial(_drop_columns_kernel, col0 // tile, gap),
        grid=(rows // tr, n_out // tile),
        in_specs=[pl.BlockSpec((tile, tr), lambda i, j: (j, i)),
                  pl.BlockSpec((gap, tr), lambda i, j: ((j + 1) * (tile // gap), i))],
        out_specs=[pl.BlockSpec((tr, tile), lambda i, j: (i, j)), pl.BlockSpec((tr, LANES), lambda i, j: (i, 0))],
        out_shape=[jax.ShapeDtypeStruct((rows, n_out), BF16), jax.ShapeDtypeStruct((rows, LANES), BF16)],
        compiler_params=_params(("arbitrary", "arbitrary")),
        name="in_proj_weights",
    )(wt, wt)


def _dn_gates_kernel(chunk, nh, h_ref, w_ref, wt_ref, alog_ref, dtb_ref, alogt_ref, dtbt_ref, col_ref, row_ref):
    tm = h_ref.shape[0]
    h = h_ref[...]
    zc = jnp.dot(h, w_ref[...], preferred_element_type=F32)
    zr = lax.dot_general(wt_ref[...], h, (((1,), (1,)), ((), ())), preferred_element_type=F32)
    r = lax.broadcasted_iota(I32, (tm, tm), 0)
    c = lax.broadcasted_iota(I32, (tm, tm), 1)
    same = (r // chunk) == (c // chunk)
    le = jnp.where(same & (c <= r), 1.0, 0.0).astype(F32)
    ge = jnp.where(same & (c >= r), 1.0, 0.0).astype(F32)
    hp = lax.Precision.HIGHEST
    for d in range(2):
        a_c = zc[:, d * nh:(d + 1) * nh]
        b_c = zc[:, (2 + d) * nh:(3 + d) * nh]
        g_c = -jnp.exp(alog_ref[:, d * nh:(d + 1) * nh]) * _softplus(a_c + dtb_ref[:, d * nh:(d + 1) * nh])
        a_r = zr[d * nh:(d + 1) * nh, :]
        b_r = zr[(2 + d) * nh:(3 + d) * nh, :]
        g_r = -jnp.exp(alogt_ref[d * nh:(d + 1) * nh, :]) * _softplus(a_r + dtbt_ref[d * nh:(d + 1) * nh, :])
        incl, rest = (le, ge) if d == 0 else (ge, le)
        gc_c = jnp.dot(incl, g_c, preferred_element_type=F32, precision=hp)
        gr_c = jnp.dot(rest, g_c, preferred_element_type=F32, precision=hp) - g_c
        gc_r = jnp.dot(g_r, rest, preferred_element_type=F32, precision=hp)
        gr_r = jnp.dot(g_r, incl, preferred_element_type=F32, precision=hp) - g_r
        col_ref[d] = jnp.concatenate([gc_c, _sigmoid(b_c), gr_c], axis=1)
        row_ref[d] = jnp.concatenate([gc_r, _sigmoid(b_r), gr_r], axis=0)


def dn_gates(h_bf, w_ab, a_log, dt_bias, nh, chunk, tm=256):
    t, d = h_bf.shape
    w_t = w_ab.T
    alog = a_log.reshape(1, 2 * nh).astype(F32)
    dtb = dt_bias.reshape(1, 2 * nh).astype(F32)
    full = lambda shape: pl.BlockSpec(shape, lambda i: (0,) * len(shape))
    return pl.pallas_call(
        functools.partial(_dn_gates_kernel, chunk, nh),
        grid=(t // tm,),
        in_specs=[pl.BlockSpec((tm, d), lambda i: (i, 0)), full((d, 4 * nh)), full((4 * nh, d)),
                  full((1, 2 * nh)), full((1, 2 * nh)), full((2 * nh, 1)), full((2 * nh, 1))],
        out_specs=[pl.BlockSpec((2, tm, 3 * nh), lambda i: (0, i, 0)),
                   pl.BlockSpec((2, 3 * nh, tm), lambda i: (0, 0, i))],
        out_shape=[jax.ShapeDtypeStruct((2, t, 3 * nh), F32), jax.ShapeDtypeStruct((2, 3 * nh, t), F32)],
        compiler_params=_params(("arbitrary",)),
        name="dn_gates",
    )(h_bf, w_ab, w_t, alog, dtb, alog.T, dtb.T)


def _seq_edges(tok0, tm, seq_a, n_a, total):
    end = tok0 + tm
    starts = ((tok0 % seq_a == 0) & (tok0 <= n_a))
    ends = ((end % seq_a == 0) & (end <= n_a)) | (end == total)
    return starts, ends


def _shifted(x, prev_row, next_row):
    tm = x.shape[0]
    rows = lax.broadcasted_iota(I32, x.shape, 0)
    xm = jnp.where(rows == 0, prev_row, pltpu.roll(x, 1, 0))
    xp = jnp.where(rows == tm - 1, next_row, pltpu.roll(x, tm - 1, 0))
    return xm, xp


def _halo_specs(tm, width, col_block, n_rows):
    per = tm // HALO
    last = n_rows // HALO - 1
    return [
        pl.BlockSpec((tm, width), lambda i, *_: (i, col_block(i, *_))),
        pl.BlockSpec((HALO, width), lambda i, *_: (jnp.maximum(i * per - 1, 0), col_block(i, *_))),
        pl.BlockSpec((HALO, width), lambda i, *_: (jnp.minimum((i + 1) * per, last), col_block(i, *_))),
    ]


def _dn_prep_kernel(seq_a, n_a, total, hd, x_ref, xprev_ref, xnext_ref, w_ref, o_ref):
    i = pl.program_id(0)
    j = pl.program_id(1)
    tm, width = x_ref.shape
    starts, ends = _seq_edges(i * tm, tm, seq_a, n_a, total)
    x = x_ref[...].astype(F32)
    prev_row = jnp.where(starts, 0.0, xprev_ref[HALO - 1:HALO, :].astype(F32))
    next_row = jnp.where(ends, 0.0, xnext_ref[0:1, :].astype(F32))
    xm, xp = _shifted(x, prev_row, next_row)
    y = _silu(xm * w_ref[0:1, :] + x * w_ref[1:2, :] + xp * w_ref[2:3, :])

    @pl.when(j == 2)
    def _():
        o_ref[...] = y.astype(o_ref.dtype)

    @pl.when(j < 2)
    def _():
        scale = jnp.where(j == 0, hd ** -0.5, 1.0).astype(F32)
        for h in range(width // hd):
            yh = y[:, h * hd:(h + 1) * hd]
            inv = lax.rsqrt(jnp.sum(yh * yh, -1, keepdims=True) + L2_EPS) * scale
            o_ref[:, h * hd:(h + 1) * hd] = (yh * inv).astype(o_ref.dtype)


def dn_prep(z, w_conv, width, hd, seq_a, n_a, tm=256):
    t = z.shape[0]
    return pl.pallas_call(
        functools.partial(_dn_prep_kernel, seq_a, n_a, t, hd),
        grid=(t // tm, 3),
        in_specs=_halo_specs(tm, width, lambda i, j: j, t) + [pl.BlockSpec((3, width), lambda i, j: (0, j))],
        out_specs=pl.BlockSpec((tm, width), lambda i, j: (i, j)),
        out_shape=jax.ShapeDtypeStruct((t, 3 * width), BF16),
        compiler_params=_params(("arbitrary", "arbitrary")),
        name="dn_prep",
    )(z, z, z, w_conv)


def _delta_kernel(seq_chunks_a, chunks_a, n_chunks, nh, hd, *refs):
    in_f, in_b = refs[0:5], refs[5:10]
    of_ref, ob_ref, sf_ref, sb_ref = refs[10:14]
    i = pl.program_id(0)
    cb = n_chunks - 1 - i

    @pl.when((i % seq_chunks_a == 0) & (i <= chunks_a))
    def _():
        sf_ref[...] = jnp.zeros_like(sf_ref)

    @pl.when((((cb + 1) % seq_chunks_a == 0) & (cb + 1 <= chunks_a)) | (cb + 1 == n_chunks))
    def _():
        sb_ref[...] = jnp.zeros_like(sb_ref)

    out_f, st_f = _delta_chunk(False, nh, hd, *in_f, sf_ref)
    out_b, st_b = _delta_chunk(True, nh, hd, *in_b, sb_ref)
    of_ref[...] = out_f
    ob_ref[...] = out_b
    for h in range(nh):
        sf_ref[h] = st_f[h]
        sb_ref[h] = st_b[h]


def _delta_chunk(reverse, nh, hd, q_ref, k_ref, v_ref, gcol_ref, grow_ref, s_ref):
    cs = q_ref.shape[0]
    r = lax.broadcasted_iota(I32, (cs, 2 * cs), 0)
    lane = lax.broadcasted_iota(I32, (cs, 2 * cs), 1)
    left = lane < cs
    cc = jnp.where(left, lane, lane - cs)
    ahead = (cc - r) if reverse else (r - cc)
    tri = ahead >= 0
    strict = ahead > 0
    eye = (r == cc).astype(F32)
    left_w = lax.broadcasted_iota(I32, (cs, 2 * hd), 1) < hd
    dot = functools.partial(jnp.dot, preferred_element_type=F32)

    def block_diag(x, is_left):
        zero = jnp.zeros_like(x)
        return jnp.concatenate([jnp.where(is_left, x, zero), jnp.where(is_left, zero, x)], axis=0)

    nt_dims = (((1,), (1,)), ((), ()))
    tn_dims = (((0,), (0,)), ((), ()))
    n_sq = max(1, (cs - 1).bit_length() - 1)
    pairs = list(range(0, nh, 2))
    heads = list(range(nh))
    cols = [slice(h * hd, (h + 1) * hd) for h in heads]
    kf = [k_ref[:, s_].astype(F32) for s_ in cols]
    gc = [gcol_ref[:, h:h + 1] for h in heads]
    beta = [gcol_ref[:, nh + h:nh + h + 1] for h in heads]
    gr = [gcol_ref[:, 2 * nh + h:2 * nh + h + 1] for h in heads]
    s_old = [s_ref[h] for h in heads]
    sb = [a.astype(BF16) for a in s_old]
    eg = [jnp.exp(a) for a in gc]
    qks = [dot(jnp.concatenate([(q_ref[:, s_].astype(F32) * e).astype(BF16), (c * (b * e)).astype(BF16)], axis=0), st)
           for s_, c, b, e, st in zip(cols, kf, beta, eg, sb)]
    cols2 = [slice(h * hd, (h + 2) * hd) for h in pairs]
    k2 = [k_ref[:, s_] for s_ in cols2]
    q2 = [q_ref[:, s_] for s_ in cols2]
    gc2 = [jnp.where(left, gc[h], gc[h + 1]) for h in pairs]
    beta2 = [jnp.where(left, beta[h], beta[h + 1]) for h in pairs]
    gc_row2 = [jnp.concatenate([grow_ref[h:h + 1, :], grow_ref[h + 1:h + 2, :]], axis=1) for h in pairs]
    decay = [jnp.exp(jnp.where(tri, a - b, -jnp.inf)) for a, b in zip(gc2, gc_row2)]
    kq = [lax.dot_general(jnp.concatenate([a, b], axis=0), block_diag(a, left_w), nt_dims,
                          preferred_element_type=F32) for a, b in zip(k2, q2)]
    low = [jnp.where(strict, b * a[:cs] * dcy, 0.0) for a, b, dcy in zip(kq, beta2, decay)]
    attn = [(a[cs:] * dcy).astype(BF16) for a, dcy in zip(kq, decay)]
    inv = [eye - a for a in low]
    x = [dot(a.astype(BF16), block_diag(a, left).astype(BF16)) for a in low]
    for j in range(n_sq):
        w = [block_diag(a, left).astype(BF16) for a in x]
        if j + 1 < n_sq:
            both = [dot(jnp.concatenate([p, a], axis=0).astype(BF16), b) for p, a, b in zip(inv, x, w)]
            inv = [p + a[:cs] for p, a in zip(inv, both)]
            x = [a[cs:] for a in both]
        else:
            inv = [p + dot(p.astype(BF16), b) for p, b in zip(inv, w)]
    rhs = [jnp.concatenate([v_ref[:, cols[h]].astype(F32) * beta[h] - qks[h][cs:],
                            v_ref[:, cols[h + 1]].astype(F32) * beta[h + 1] - qks[h + 1][cs:]], axis=1)
           for h in pairs]
    v_new = [dot(p.astype(BF16), block_diag(a, left_w).astype(BF16)) for p, a in zip(inv, rhs)]
    vb = [a.astype(BF16) for a in v_new]
    outs = [jnp.concatenate([qks[h][:cs], qks[h + 1][:cs]], axis=1) + dot(a, block_diag(b, left_w))
            for h, a, b in zip(pairs, attn, vb)]
    kd = [(a * jnp.exp(b)).astype(BF16) for a, b in zip(kf, gr)]
    g_tot = [jnp.exp(a[0:1, :] + b[0:1, :]) for a, b in zip(gc, gr)]
    states = [s_old[h] * g_tot[h]
              + lax.dot_general(kd[h], vb[h // 2][:, (h % 2) * hd:(h % 2 + 1) * hd], tn_dims,
                                preferred_element_type=F32) for h in heads]
    return jnp.concatenate(outs, axis=1), states


def delta_rule(qkv, gcol, grow, nh, hd, seq_a, n_a):
    t = qkv.shape[0]
    width = nh * hd
    cs = DN_CHUNK
    n_chunks = t // cs
    ng = gcol.shape[-1]
    assert nh % 2 == 0 and 2 * cs == LANES
    grow = grow.reshape(2, ng, n_chunks, cs).transpose(0, 2, 1, 3)

    def specs(d):
        chunk = (lambda i: i) if d == 0 else (lambda i: n_chunks - 1 - i)
        return [pl.BlockSpec((cs, width), lambda i: (chunk(i), 0)),
                pl.BlockSpec((cs, width), lambda i: (chunk(i), 1)),
                pl.BlockSpec((cs, width), lambda i: (chunk(i), 2)),
                pl.BlockSpec((None, cs, ng), lambda i: (d, chunk(i), 0)),
                pl.BlockSpec((None, None, ng, cs), lambda i: (d, chunk(i), 0, 0))]

    out = jax.ShapeDtypeStruct((t, width), F32)
    return pl.pallas_call(
        functools.partial(_delta_kernel, seq_a // cs, n_a // cs, n_chunks, nh, hd),
        grid=(n_chunks,),
        in_specs=specs(0) + specs(1),
        out_specs=[pl.BlockSpec((cs, width), lambda i: (i, 0)),
                   pl.BlockSpec((cs, width), lambda i: (n_chunks - 1 - i, 0))],
        out_shape=[out, out],
        scratch_shapes=[pltpu.VMEM((nh, hd, hd), F32), pltpu.VMEM((nh, hd, hd), F32)],
        compiler_params=_params(("arbitrary",)),
        name="delta_rule",
    )(qkv, qkv, qkv, gcol, grow, qkv, qkv, qkv, gcol, grow)


def _dn_post_kernel(hd, col_block, of_ref, ob_ref, og_ref, nw_ref, y_ref):
    del col_block
    o = of_ref[...] + ob_ref[...]
    og = og_ref[...].astype(F32)
    nw = nw_ref[...]
    for h in range(o.shape[1] // hd):
        hs = slice(h * hd, (h + 1) * hd)
        oh = o[:, hs]
        inv = lax.rsqrt(jnp.mean(oh * oh, -1, keepdims=True) + RMS_EPS)
        y_ref[:, hs] = (oh * inv * nw * _silu(og[:, hs])).astype(y_ref.dtype)


def dn_post(o_f, o_b, z, og_block, norm_w, hd, tm=256):
    t, width = o_f.shape
    return pl.pallas_call(
        functools.partial(_dn_post_kernel, hd, og_block),
        grid=(t // tm,),
        in_specs=[pl.BlockSpec((tm, width), lambda i: (i, 0)),
                  pl.BlockSpec((tm, width), lambda i: (i, 0)),
                  pl.BlockSpec((tm, width), lambda i: (i, og_block)),
                  pl.BlockSpec((1, hd), lambda i: (0, 0))],
        out_specs=pl.BlockSpec((tm, width), lambda i: (i, 0)),
        out_shape=jax.ShapeDtypeStruct((t, width), BF16),
        compiler_params=_params(("arbitrary",)),
        name="dn_post",
    )(o_f, o_b, z, norm_w.reshape(1, hd).astype(F32))


def _sc_kernel(seq_a, n_a, total, b_ref, c_ref, cprev_ref, cnext_ref, h_ref, hprev_ref, hnext_ref, w_ref, y_ref):
    i = pl.program_id(0)
    tm = b_ref.shape[0]
    starts, ends = _seq_edges(i * tm, tm, seq_a, n_a, total)
    x = c_ref[...].astype(F32) * h_ref[...].astype(F32)
    prev_row = jnp.where(starts, 0.0, cprev_ref[HALO - 1:HALO, :].astype(F32) * hprev_ref[HALO - 1:HALO, :].astype(F32))
    next_row = jnp.where(ends, 0.0, cnext_ref[0:1, :].astype(F32) * hnext_ref[0:1, :].astype(F32))
    xm, xp = _shifted(x, prev_row, next_row)
    conv = xm * w_ref[0:1, :] + x * w_ref[1:2, :] + xp * w_ref[2:3, :]
    y_ref[...] = (b_ref[...].astype(F32) * conv).astype(y_ref.dtype)


def short_conv(z, w_conv, first_block, width, seq_a, n_a, tm=256):
    t = z.shape[0]
    return pl.pallas_call(
        functools.partial(_sc_kernel, seq_a, n_a, t),
        grid=(t // tm,),
        in_specs=([pl.BlockSpec((tm, width), lambda i: (i, first_block))]
                  + _halo_specs(tm, width, lambda i: first_block + 1, t)
                  + _halo_specs(tm, width, lambda i: first_block + 2, t)
                  + [pl.BlockSpec((3, width), lambda i: (0, 0))]),
        out_specs=pl.BlockSpec((tm, width), lambda i: (i, 0)),
        out_shape=jax.ShapeDtypeStruct((t, width), BF16),
        compiler_params=_params(("arbitrary",)),
        name="short_conv",
    )(z, z, z, z, z, z, z, w_conv)


def _xattn_kernel(scale, q_ref, k_ref, v_ref, o_ref):
    s = lax.dot_general(q_ref[...], k_ref[...], (((1,), (1,)), ((), ())), preferred_element_type=F32) * scale
    m = jnp.max(s, -1, keepdims=True)
    e = jnp.exp(s - m)
    p = e / jnp.sum(e, -1, keepdims=True)
    o_ref[...] = jnp.dot(p.astype(BF16), v_ref[...], preferred_element_type=F32).astype(o_ref.dtype)


def mem_attention(z, kv, q_block0, n_heads, hd, n_mem, seq_a, n_a, tm=512):
    t = z.shape[0]
    n_seq_a = n_a // seq_a

    def seq_of(i):
        tok = i * tm
        return jnp.where(tok < n_a, tok // seq_a, n_seq_a)

    return pl.pallas_call(
        functools.partial(_xattn_kernel, hd ** -0.5),
        grid=(t // tm, n_heads),
        in_specs=[pl.BlockSpec((tm, hd), lambda i, h: (i, q_block0 + h)),
                  pl.BlockSpec((n_mem, hd), lambda i, h: (seq_of(i), h)),
                  pl.BlockSpec((n_mem, hd), lambda i, h: (seq_of(i), n_heads + h))],
        out_specs=pl.BlockSpec((tm, hd), lambda i, h: (i, h)),
        out_shape=jax.ShapeDtypeStruct((t, n_heads * hd), BF16),
        compiler_params=_params(("arbitrary", "arbitrary")),
        name="mem_attention",
    )(z, kv, kv)


def _merge_kernel(nb, *refs):
    y_refs = refs[:nb]
    g_refs = refs[nb:2 * nb]
    w_ref = refs[2 * nb]
    o_ref = refs[2 * nb + 1]
    acc = None
    for b in range(nb):
        term = _sigmoid(g_refs[b][...].astype(F32)) * jnp.dot(y_refs[b][...], w_ref[b], preferred_element_type=F32)
        acc = term if acc is None else acc + term
    o_ref[...] = acc.astype(o_ref.dtype)


def gated_merge(ys, z, gate_col0, w_branch, tm=1024, tn=512):
    nb = len(ys)
    t, width = ys[0].shape
    d = w_branch.shape[-1]
    g0 = gate_col0 // tn
    per = d // tn
    gate_spec = lambda b: pl.BlockSpec((tm, tn), lambda i, j: (i, g0 + b * per + j))
    return pl.pallas_call(
        functools.partial(_merge_kernel, nb),
        grid=(t // tm, d // tn),
        in_specs=([pl.BlockSpec((tm, width), lambda i, j: (i, 0))] * nb
                  + [gate_spec(b) for b in range(nb)]
                  + [pl.BlockSpec((nb, width, tn), lambda i, j: (0, 0, j))]),
        out_specs=pl.BlockSpec((tm, tn), lambda i, j: (i, j)),
        out_shape=jax.ShapeDtypeStruct((t, d), BF16),
        compiler_params=_params(("arbitrary", "arbitrary")),
        name="gated_merge",
    )(*ys, *([z] * nb), w_branch)


def _res_ln1_kernel(x_ref, y_ref, g_ref, b_ref, of_ref, op_ref):
    h = _layer_norm(DEEPNORM_ALPHA * x_ref[...] + y_ref[...], g_ref[...], b_ref[...])
    of_ref[...] = h
    half = h.shape[1] // 2
    op_ref[...] = _pack_bf16_pair(h[:, :half], h[:, half:])


def res_ln1(x, y, g, b, tm=256):
    t, d = x.shape
    row = lambda w: pl.BlockSpec((tm, w), lambda i: (i, 0))
    vec = pl.BlockSpec((1, d), lambda i: (0, 0))
    return pl.pallas_call(
        _res_ln1_kernel,
        grid=(t // tm,),
        in_specs=[row(d), row(d), vec, vec],
        out_specs=[row(d), row(d // 2)],
        out_shape=[jax.ShapeDtypeStruct((t, d), F32), jax.ShapeDtypeStruct((t, d // 2), U32)],
        compiler_params=_params(("arbitrary",)),
        name="res_ln1",
    )(x, y, g.reshape(1, d), b.reshape(1, d))


def _first_index_of_max(x, idx, big):
    m = jnp.max(x, axis=0, keepdims=True)
    return m, jnp.min(jnp.where(x == m, idx, big), axis=0, keepdims=True)


def _router_kernel(ne, ng, kg, topk, h_ref, wt_ref, bias_ref, eidx_ref, gate_ref, rank_ref, cnt_ref, run_ref):
    i = pl.program_id(0)
    tm = h_ref.shape[0]
    per = ne // ng

    @pl.when(i == 0)
    def _():
        run_ref[...] = jnp.zeros_like(run_ref)

    logits = lax.dot_general(wt_ref[...], h_ref[...], (((1,), (1,)), ((), ())),
                             preferred_element_type=F32, precision=lax.Precision.HIGHEST)
    scores = _sigmoid(logits)
    sel = scores + bias_ref[...]
    eid = lax.broadcasted_iota(I32, (ne, tm), 0)
    neg = -jnp.inf
    pid = lax.broadcasted_iota(I32, (per, tm), 0)
    gscores = []
    for g in range(ng):
        sg = sel[g * per:(g + 1) * per, :]
        m1, a1 = _first_index_of_max(sg, pid, per)
        m2 = jnp.max(jnp.where(pid == a1, neg, sg), axis=0, keepdims=True)
        gscores.append(m1 + m2)
    gscore = jnp.concatenate(gscores, axis=0)
    gid = lax.broadcasted_iota(I32, (ng, tm), 0)
    egroup = eid // per
    keep_e = jnp.zeros((ne, tm), jnp.bool_)
    for _ in range(kg):
        _, a = _first_index_of_max(gscore, gid, ng)
        keep_e = keep_e | (egroup == a)
        gscore = jnp.where(gid == a, neg, gscore)
    cand = jnp.where(keep_e, sel, neg)
    onehot = jnp.zeros((ne, tm), F32)
    idxs, gates = [], []
    for _ in range(topk):
        _, a = _first_index_of_max(cand, eid, ne)
        hit = eid == a
        idxs.append(a)
        gates.append(jnp.sum(jnp.where(hit, scores, 0.0), axis=0, keepdims=True))
        onehot = jnp.where(hit, 1.0, onehot)
        cand = jnp.where(hit, neg, cand)
    gate = jnp.concatenate(gates, axis=0)
    gate = gate / jnp.sum(gate, axis=0, keepdims=True) * ROUTE_SCALE
    s_ = lax.broadcasted_iota(I32, (tm, tm), 0)
    t_ = lax.broadcasted_iota(I32, (tm, tm), 1)
    before = jnp.where(s_ < t_, 1.0, 0.0).astype(BF16)
    prior = jnp.dot(onehot.astype(BF16), before, preferred_element_type=F32) + run_ref[...]
    ranks = [jnp.sum(jnp.where(eid == a, prior, 0.0), axis=0, keepdims=True) for a in idxs]
    eidx_ref[...] = jnp.concatenate(idxs, axis=0)
    gate_ref[...] = gate
    rank_ref[...] = jnp.concatenate(ranks, axis=0).astype(I32)
    run_ref[...] = run_ref[...] + jnp.sum(onehot, axis=1, keepdims=True)
    cnt_ref[...] = run_ref[...].astype(I32)


def router(h_f32, w_router, bias, tm=256):
    t, d = h_f32.shape
    ne = w_router.shape[1]
    out = lambda dt: jax.ShapeDtypeStruct((TOP_K, t), dt)
    tok = pl.BlockSpec((TOP_K, tm), lambda i: (0, i))
    return pl.pallas_call(
        functools.partial(_router_kernel, ne, N_EXPERT_GROUPS, TOPK_GROUPS, TOP_K),
        grid=(t // tm,),
        in_specs=[pl.BlockSpec((tm, d), lambda i: (i, 0)), pl.BlockSpec((ne, d), lambda i: (0, 0)),
                  pl.BlockSpec((ne, 1), lambda i: (0, 0))],
        out_specs=[tok, tok, tok, pl.BlockSpec((ne, 1), lambda i: (0, 0))],
        out_shape=[out(I32), out(F32), out(I32), jax.ShapeDtypeStruct((ne, 1), I32)],
        scratch_shapes=[pltpu.VMEM((ne, 1), F32)],
        compiler_params=_params(("arbitrary",)),
        name="router",
    )(h_f32, w_router.T.astype(F32), bias.reshape(ne, 1).astype(F32))


def _zero_blocks_kernel(blk_ref, o_ref):
    del blk_ref
    o_ref[...] = jnp.zeros_like(o_ref)


def zero_blocks(block_ids, n_rows, w, tm):
    return pl.pallas_call(
        _zero_blocks_kernel,
        grid_spec=pltpu.PrefetchScalarGridSpec(
            num_scalar_prefetch=1,
            grid=(block_ids.shape[0],),
            in_specs=[],
            out_specs=pl.BlockSpec((tm, w), lambda e, blk: (blk[e], 0)),
        ),
        out_shape=jax.ShapeDtypeStruct((n_rows, w), U32),
        compiler_params=_params(("arbitrary",)),
        name="moe_pad_zero",
    )(block_ids)


def _dispatch_kernel(topk, pos_ref, h_ref, dst_in_ref, dst_ref, sem):
    del dst_in_ref
    tb = h_ref.shape[0]

    def issue(r, carry):
        for k in range(topk):
            pltpu.make_async_copy(h_ref.at[pl.ds(r, 1), :], dst_ref.at[pl.ds(pos_ref[0, k, r], 1), :], sem).start()
        return carry

    lax.fori_loop(0, tb, issue, 0, unroll=True)
    for k in range(topk):
        pltpu.make_async_copy(h_ref, dst_ref.at[pl.ds(0, tb), :], sem).wait()


def moe_dispatch(h_pk, pos3, dst, tb):
    t, w = h_pk.shape
    nb, topk, _ = pos3.shape
    return pl.pallas_call(
        functools.partial(_dispatch_kernel, topk),
        grid=(nb,),
        in_specs=[pl.BlockSpec((1, topk, tb), lambda i: (i, 0, 0), memory_space=pltpu.SMEM),
                  pl.BlockSpec((tb, w), lambda i: (i, 0)),
                  pl.BlockSpec(memory_space=pl.ANY)],
        out_specs=pl.BlockSpec(memory_space=pl.ANY),
        out_shape=jax.ShapeDtypeStruct(dst.shape, dst.dtype),
        input_output_aliases={2: 0},
        scratch_shapes=[pltpu.SemaphoreType.DMA(())],
        compiler_params=_params(("arbitrary",)),
        name="moe_dispatch",
    )(pos3, h_pk, dst)


GMM_K_CHUNKS = 4


def _gmm_up_kernel(be_ref, nb_ref, x_ref, wg_ref, wu_ref, o_ref):
    b = pl.program_id(1)

    @pl.when(b < nb_ref[0])
    def _():
        words = x_ref.shape[1]
        kc = 2 * words // GMM_K_CHUNKS
        per_half = GMM_K_CHUNKS // 2
        g = u = None
        for c in range(GMM_K_CHUNKS):
            w = x_ref[:, (c % per_half) * kc:(c % per_half + 1) * kc]
            bits = (w << 16) if c < per_half else (w & jnp.uint32(0xFFFF0000))
            xc = lax.bitcast_convert_type(bits, F32).astype(BF16)
            rows = slice(c * kc, (c + 1) * kc)
            gc = jnp.dot(xc, wg_ref[rows, :].astype(BF16), preferred_element_type=F32)
            uc = jnp.dot(xc, wu_ref[rows, :].astype(BF16), preferred_element_type=F32)
            g = gc if g is None else g + gc
            u = uc if u is None else u + uc
        o_ref[...] = (_silu(g) * u).astype(o_ref.dtype)

    @pl.when(b >= nb_ref[0])
    def _():
        o_ref[...] = jnp.zeros_like(o_ref)


def gmm_up(x, w_gate, w_up, block_expert, n_used, tm, name, tn=512):
    rows, xw = x.shape
    _, d, f = w_gate.shape
    tn = _tile(f, tn)
    assert d == 2 * xw and (d // GMM_K_CHUNKS) % LANES == 0
    wspec = pl.BlockSpec((None, d, tn), lambda j, b, be, nb: (be[b], 0, j))
    return pl.pallas_call(
        _gmm_up_kernel,
        grid_spec=pltpu.PrefetchScalarGridSpec(
            num_scalar_prefetch=2,
            grid=(f // tn, rows // tm),
            in_specs=[pl.BlockSpec((tm, xw), lambda j, b, be, nb: (jnp.minimum(b, nb[0] - 1), 0)), wspec, wspec],
            out_specs=pl.BlockSpec((tm, tn), lambda j, b, be, nb: (b, j)),
        ),
        out_shape=jax.ShapeDtypeStruct((rows, f), BF16),
        compiler_params=_params(("arbitrary", "arbitrary")),
        name=name,
    )(block_expert, n_used, x, w_gate, w_up)


def _gmm_down_kernel(n_ranges, be_ref, nb_ref, h_ref, w_ref, o_ref):
    b = pl.program_id(1)

    @pl.when(b < nb_ref[0])
    def _():
        h = h_ref[...]
        cw = w_ref.shape[1] // n_ranges
        for c in range(n_ranges):
            y = jnp.dot(h, w_ref[:, c * cw:(c + 1) * cw].astype(BF16), preferred_element_type=F32)
            o_ref[:, c * (cw // 2):(c + 1) * (cw // 2)] = _pack_bf16_pair(y[:, :cw // 2], y[:, cw // 2:])

    @pl.when(b >= nb_ref[0])
    def _():
        o_ref[...] = jnp.zeros_like(o_ref)


def down_ranges(d):
    return max(r for r in (4, 2, 1) if (d // r) % (2 * LANES) == 0)


def gmm_down(h, w_down, block_expert, n_used, tm, name):
    rows, f = h.shape
    _, _, d = w_down.shape
    return pl.pallas_call(
        functools.partial(_gmm_down_kernel, down_ranges(d)),
        grid_spec=pltpu.PrefetchScalarGridSpec(
            num_scalar_prefetch=2,
            grid=(1, rows // tm),
            in_specs=[pl.BlockSpec((tm, f), lambda j, b, be, nb: (jnp.minimum(b, nb[0] - 1), 0)),
                      pl.BlockSpec((None, f, d), lambda j, b, be, nb: (be[b], 0, 0))],
            out_specs=pl.BlockSpec((tm, d // 2), lambda j, b, be, nb: (b, 0)),
        ),
        out_shape=jax.ShapeDtypeStruct((rows, d // 2), U32),
        compiler_params=_params(("arbitrary", "arbitrary")),
        name=name,
    )(block_expert, n_used, h, w_down)


def _unpack_rows(w, n_split):
    q = w.shape[1] // n_split
    parts = []
    for s in range(n_split):
        lo, hi = _unpack_bf16_pair(w[:, s * q:(s + 1) * q])
        parts += [lo, hi]
    return jnp.concatenate(parts, axis=1)


COMBINE_DEPTH = 3


def _combine_kernel(topk, n_split, half_blocks, pos_ref, pos1_ref, pos2_ref, y_ref, h_ref, sh_ref, gate_ref, g_ref,
                    b_ref, op_ref, os_ref, *scratch):
    bufs, sem = scratch[:COMBINE_DEPTH], scratch[COMBINE_DEPTH]
    i = pl.program_id(0)
    n = pl.num_programs(0)
    tb = h_ref.shape[0]

    def issue(p_ref, s):
        def body(r, carry):
            for k in range(topk):
                pltpu.make_async_copy(y_ref.at[pl.ds(p_ref[0, k, r], 1), :], bufs[s].at[k, pl.ds(r, 1), :],
                                      sem.at[s]).start()
            return carry
        lax.fori_loop(0, tb, body, 0, unroll=True)

    def drain(s):
        for k in range(topk):
            pltpu.make_async_copy(y_ref.at[pl.ds(0, tb), :], bufs[s].at[k], sem.at[s]).wait()

    @pl.when(i == 0)
    def _():
        issue(pos_ref, 0)
        issue(pos1_ref, 1)

    def step(s):
        drain(s)
        issue(pos2_ref, (s + 2) % COMBINE_DEPTH)
        acc = DEEPNORM_ALPHA * h_ref[...] + _unpack_rows(sh_ref[...], n_split)
        for k in range(topk):
            acc = acc + gate_ref[:, k:k + 1] * _unpack_rows(bufs[s][k], n_split)
        out = _layer_norm(acc, g_ref[...], b_ref[...])

        @pl.when(i < half_blocks)
        def _():
            op_ref[...] = out

        @pl.when(i >= half_blocks)
        def _():
            os_ref[...] = out

        @pl.when(i == n - 1)
        def _():
            drain((s + 1) % COMBINE_DEPTH)
            drain((s + 2) % COMBINE_DEPTH)

    for s in range(COMBINE_DEPTH):
        pl.when(i % COMBINE_DEPTH == s)(functools.partial(step, s))


def moe_combine(y_sorted, pos3, gate, h_f32, shared, g, b, n_first, n_split):
    t, d = h_f32.shape
    nb, topk, tb = pos3.shape
    hb = n_first // tb
    smem = lambda f: pl.BlockSpec((1, topk, tb), f, memory_space=pltpu.SMEM)
    row = lambda w: pl.BlockSpec((tb, w), lambda i: (i, 0))
    vec = pl.BlockSpec((1, d), lambda i: (0, 0))
    return pl.pallas_call(
        functools.partial(_combine_kernel, topk, n_split, hb),
        grid=(nb,),
        in_specs=[smem(lambda i: (i, 0, 0)), smem(lambda i: (jnp.minimum(i + 1, nb - 1), 0, 0)),
                  smem(lambda i: (jnp.minimum(i + 2, nb - 1), 0, 0)),
                  pl.BlockSpec(memory_space=pl.ANY), row(d), row(d // 2), row(topk), vec, vec],
        out_specs=[pl.BlockSpec((tb, d), lambda i: (jnp.minimum(i, hb - 1), 0)),
                   pl.BlockSpec((tb, d), lambda i: (jnp.maximum(i - hb, 0), 0))],
        out_shape=[jax.ShapeDtypeStruct((n_first, d), F32), jax.ShapeDtypeStruct((t - n_first, d), F32)],
        scratch_shapes=([pltpu.VMEM((topk, tb, d // 2), U32)] * COMBINE_DEPTH
                        + [pltpu.SemaphoreType.DMA((COMBINE_DEPTH,))]),
        compiler_params=_params(("arbitrary",)),
        name="moe_combine",
    )(pos3, pos3, pos3, y_sorted, h_f32, shared, gate, g.reshape(1, d), b.reshape(1, d))


MOE_TM = 512
MOE_TB = 64


def _moe_schedule(eidx, rank, counts, tm, tb):
    topk, t = eidx.shape
    ne = counts.shape[0]
    n_blocks = (t * topk + ne * (tm - 1) + tm - 1) // tm
    padded = (counts + tm - 1) // tm * tm
    ends = jnp.cumsum(padded)
    starts = ends - padded
    start_of = jnp.sum(jnp.where(eidx[:, :, None] == jnp.arange(ne, dtype=I32), starts.astype(I32), 0), -1)
    pos = (start_of + rank).astype(I32)
    pos3 = pos.reshape(topk, t // tb, tb).transpose(1, 0, 2)
    n_used = (ends[-1] // tm).astype(I32)
    blk = jnp.arange(n_blocks, dtype=I32)
    blk_e = jnp.searchsorted(ends, jnp.minimum(blk, n_used - 1) * tm, side="right").astype(I32)
    blk_e = jnp.minimum(blk_e, ne - 1)
    pad_blk = jnp.where(counts > 0, ends // tm - 1, jnp.minimum(starts // tm, n_blocks - 1)).astype(I32)
    return pos3, blk_e, n_used.reshape(1), pad_blk, n_blocks * tm


def kernel(x_prompt, x_sample, mem_prompt, mem_sample, ln_in_g, ln_in_b, w_in, w_dn_conv, dn_a_log, dn_dt_bias,
           dn_norm_w, w_sc_conv, w_mem_kv, w_branch, w_o, ln1_g, ln1_b, w_router, router_bias, w_gate_e, w_up_e,
           w_down_e, w_gate_s, w_up_s, w_down_s, ln2_g, ln2_b):
    assert w_in.shape[0] == DEPTH == 1
    bp, sp, d = x_prompt.shape
    bs, ss, _ = x_sample.shape
    assert bs == 1
    n_a, seq_a = bp * sp, sp
    t = n_a + bs * ss
    nh = dn_a_log.shape[-1]
    hd = dn_norm_w.shape[-1]
    dn_w = nh * hd
    sc_w = w_sc_conv.shape[-1]
    x_w = w_mem_kv.shape[-1] // 2
    n_mem = mem_prompt.shape[1]
    nbr = w_branch.shape[1]
    assert dn_w == sc_w == x_w, "column blocks of the combined projection are addressed in units of one width"
    blk = dn_w
    lyr = 0

    ab0 = 4 * dn_w
    w_main, w_cut = drop_columns(w_in, lyr, ab0, 4 * nh)
    w_ab = w_cut[:, :4 * nh]

    h0_f, h0_b = ln_in(x_prompt.reshape(n_a, d), x_sample.reshape(t - n_a, d), ln_in_g, ln_in_b)
    z = matmul(h0_b, w_main, 1024, 1024, BF16, "in_proj")

    gcol, grow = dn_gates(h0_b, w_ab, dn_a_log[lyr], dn_dt_bias[lyr], nh, DN_CHUNK)
    qkv = dn_prep(z, w_dn_conv[lyr].astype(F32), dn_w, hd, seq_a, n_a)
    o_f, o_b = delta_rule(qkv, gcol, grow, nh, hd, seq_a, n_a)
    y_dn = dn_post(o_f, o_b, z, 3, dn_norm_w[lyr], hd)
    y_sc = short_conv(z, w_sc_conv[lyr].astype(F32), 4, sc_w, seq_a, n_a)
    mem = jnp.concatenate([mem_prompt.reshape(bp * n_mem, d), mem_sample.reshape(bs * n_mem, d)], 0).astype(BF16)
    kv = matmul(mem, w_mem_kv[lyr].astype(BF16), 256, 1024, BF16, "mem_kv")
    xhd = x_w // N_X_HEADS
    y_mem = mem_attention(z, kv, 7 * blk // xhd, N_X_HEADS, xhd, n_mem, seq_a, n_a)

    mixed = gated_merge([y_dn, y_sc, y_mem], z, 8 * blk, w_branch[lyr].astype(BF16))
    attn_out = matmul(mixed, w_o[lyr].astype(BF16), 1024, 1024, F32, "out_proj")
    h_f, h_pk = res_ln1(h0_f, attn_out, ln1_g[lyr], ln1_b[lyr])

    eidx, gate, rank, counts = router(h_f, w_router[lyr], router_bias[lyr])
    pos3, blk_e, n_used, pad_blk, n_rows = _moe_schedule(eidx, rank, counts.reshape(-1), MOE_TM, MOE_TB)
    x_sorted = moe_dispatch(h_pk, pos3, zero_blocks(pad_blk, n_rows, d // 2, MOE_TM), MOE_TB)
    hid = gmm_up(x_sorted, w_gate_e[lyr], w_up_e[lyr], blk_e, n_used, MOE_TM, "moe_up")
    y_sorted = gmm_down(hid, w_down_e[lyr], blk_e, n_used, MOE_TM, "moe_down")
    one = jnp.zeros((t // MOE_TM,), I32)
    all_blocks = jnp.full((1,), t // MOE_TM, I32)
    hid_s = gmm_up(h_pk, w_gate_s, w_up_s, one, all_blocks, MOE_TM, "shared_up")
    y_shared = gmm_down(hid_s, w_down_s, one, all_blocks, MOE_TM, "shared_down")
    y_p, y_s = moe_combine(y_sorted, pos3, gate.T, h_f, y_shared, ln2_g[lyr], ln2_b[lyr], n_a, down_ranges(d))
    return y_p.reshape(bp, sp, d), y_s.reshape(bs, ss, d)
```
